```python
import math
import jax, jax.numpy as jnp
from jax import lax
import numpy as np

D_MODEL = 1024
BATCH = 4
SEQ = 8192
DEPTH = 2

N_MIXERS = 2
ATT_HEADS = 16
ATT_HEAD_DIM = D_MODEL // ATT_HEADS
MOBA_BLOCK = 256
MOBA_TOPK = 3
MOBA_Q_CHUNK = 64
RET_HEADS = 4
RET_KEY_DIM = D_MODEL // RET_HEADS
RET_VAL_DIM = 2 * RET_KEY_DIM
RET_CHUNK = 128
FFN_DIM = 2816
CONV_WIDTH = 3
LN_EPS = 1e-5
GN_EPS = 1e-5
DEEPNORM_ALPHA = (2.0 * DEPTH) ** 0.25
DEEPNORM_BETA = (8.0 * DEPTH) ** -0.25
N_ATT_LAYERS = (DEPTH + 1) // 2
N_RET_LAYERS = DEPTH // 2
NEG_INF = -1e30

kernel_name = "moba_retnet_interleaved_deepnorm_convffn"


def layer_norm(x, g, b):
    xf = x.astype(jnp.float32)
    mu = jnp.mean(xf, axis=-1, keepdims=True)
    var = jnp.mean(jnp.square(xf - mu), axis=-1, keepdims=True)
    y = (xf - mu) * lax.rsqrt(var + LN_EPS)
    return (y * g + b).astype(x.dtype)


def moba_attention(x, w_in, w_out):
    B, S, _ = x.shape
    H, hd = ATT_HEADS, ATT_HEAD_DIM
    QC, BLK = MOBA_Q_CHUNK, MOBA_BLOCK
    q, k, v = jnp.split(x @ w_in, 3, axis=-1)

    def heads(t):
        return t.reshape(B, S, H, hd).transpose(0, 2, 1, 3)

    q = heads(q) * (hd ** -0.5)
    k, v = heads(k), heads(v)
    n_blk = -(-S // BLK)
    s_pad = n_blk * BLK
    pad = ((0, 0), (0, 0), (0, s_pad - S), (0, 0))
    k_blk = jnp.pad(k, pad).reshape(B, H, n_blk, BLK, hd)
    v_blk = jnp.pad(v, pad).reshape(B, H, n_blk, BLK, hd)
    k_mean = jnp.mean(k_blk.astype(jnp.float32), axis=3).astype(k.dtype)
    n_q = S // QC
    q_chunks = q.reshape(B, H, n_q, QC, hd)
    top_k = min(MOBA_TOPK, n_blk)
    slopes = jnp.exp2(-8.0 * jnp.arange(1, H + 1, dtype=jnp.float32) / H)
    h_idx = jnp.arange(H)[:, None, None]
    blk_ids = jnp.arange(n_blk)
    offs = jnp.arange(BLK)

    def one_chunk(n):
        b = n // n_q
        c = n % n_q
        qc = q_chunks[b, :, c]
        t = c * QC + jnp.arange(QC)
        j = (c * QC) // BLK
        kb, vb, km = k_blk[b], v_blk[b], k_mean[b]
        gate = jnp.einsum('hqd,hnd->hqn', qc, km).astype(jnp.float32)
        gate = jnp.where(blk_ids < j, gate, NEG_INF)
        _, sel = lax.top_k(gate, top_k)
        sel_ok = jnp.arange(top_k) < j
        sel = jnp.where(sel_ok, sel, 0)
        k_sel = kb[h_idx, sel]
        v_sel = vb[h_idx, sel]
        s_sel = jnp.einsum('hqd,hqrkd->hqrk', qc, k_sel).astype(jnp.float32)
        pos_sel = sel[..., None] * BLK + offs
        dist_sel = (t[None, :, None, None] - pos_sel).astype(jnp.float32)
        s_sel = s_sel - slopes[:, None, None, None] * dist_sel
        s_sel = jnp.where(sel_ok[None, None, :, None], s_sel, NEG_INF)
        s_sel = s_sel.reshape(H, QC, top_k * BLK)
        k_own = lax.dynamic_index_in_dim(kb, j, axis=1, keepdims=False)
        v_own = lax.dynamic_index_in_dim(vb, j, axis=1, keepdims=False)
        dist_own = (t[:, None] - (j * BLK + offs)[None, :]).astype(jnp.float32)
        s_own = jnp.einsum('hqd,hkd->hqk', qc, k_own).astype(jnp.float32)
        s_own = s_own - slopes[:, None, None] * dist_own
        s_own = jnp.where(dist_own >= 0, s_own, NEG_INF)
        p = jax.nn.softmax(jnp.concatenate([s_sel, s_own], axis=-1), axis=-1).astype(v.dtype)
        p_sel = p[..., :top_k * BLK].reshape(H, QC, top_k, BLK)
        p_own = p[..., top_k * BLK:]
        return (jnp.einsum('hqrk,hqrkd->hqd', p_sel, v_sel)
                + jnp.einsum('hqk,hkd->hqd', p_own, v_own))

    out = lax.map(one_chunk, jnp.arange(B * n_q))
    out = out.reshape(B, n_q, H, QC, hd).transpose(0, 1, 3, 2, 4).reshape(B, S, H * hd)
    return out @ w_out


def retention(x, w_in, w_out, gn_g, gn_b):
    B, S, _ = x.shape
    H, dk, dv, C = RET_HEADS, RET_KEY_DIM, RET_VAL_DIM, RET_CHUNK
    q, k, v, g = jnp.split(x @ w_in, [H * dk, 2 * H * dk, 2 * H * dk + H * dv], axis=-1)
    n_c = S // C

    def chunks(t, d):
        return t.reshape(B, n_c, C, H, d).transpose(1, 0, 3, 2, 4).astype(jnp.float32)

    qc = chunks(q, dk)
    kc = chunks(k, dk) * (dk ** -0.5)
    vc = chunks(v, dv)
    log_gamma = jnp.log1p(-jnp.exp2(-5.0 - jnp.arange(H, dtype=jnp.float32)))
    i = jnp.arange(C, dtype=jnp.float32)
    diff = i[:, None] - i[None, :]
    decay_in = jnp.where(diff >= 0, jnp.exp(log_gamma[:, None, None] * jnp.maximum(diff, 0.0)), 0.0)
    q_decay = jnp.exp(log_gamma[:, None] * (i + 1.0))
    k_decay = jnp.exp(log_gamma[:, None] * (C - 1.0 - i))
    chunk_decay = jnp.exp(log_gamma * C)

    def step(state, inp):
        qi, ki, vi = inp
        scores = jnp.einsum('bhqd,bhkd->bhqk', qi, ki) * decay_in
        inner = jnp.einsum('bhqk,bhkv->bhqv', scores, vi)
        cross = jnp.einsum('bhqd,bhdv->bhqv', qi * q_decay[..., None], state)
        state = (chunk_decay[:, None, None] * state
                 + jnp.einsum('bhkd,bhkv->bhdv', ki * k_decay[..., None], vi))
        return state, inner + cross

    state0 = jnp.zeros((B, H, dk, dv), jnp.float32)
    _, o = lax.scan(step, state0, (qc, kc, vc))
    o = o.transpose(1, 0, 3, 2, 4).reshape(B, S, H, dv)
    mu = jnp.mean(o, axis=-1, keepdims=True)
    var = jnp.mean(jnp.square(o - mu), axis=-1, keepdims=True)
    o = ((o - mu) * lax.rsqrt(var + GN_EPS)).reshape(B, S, H * dv)
    o = (o * gn_g + gn_b).astype(x.dtype) * jax.nn.silu(g)
    return o @ w_out


def conv_ffn(x, w_up, conv_w, conv_b, w_down):
    S = x.shape[1]
    u = x @ w_up
    u_pad = jnp.pad(u, ((0, 0), (CONV_WIDTH - 1, 0), (0, 0)))
    h = conv_b + conv_w[CONV_WIDTH - 1] * u_pad[:, CONV_WIDTH - 1:CONV_WIDTH - 1 + S]
    for w in range(CONV_WIDTH - 1):
        h = h + conv_w[w] * u_pad[:, w:w + S]
    a, gate = jnp.split(h, 2, axis=-1)
    return (jax.nn.gelu(a, approximate=False) * gate) @ w_down


def setup_inputs(seed: int = 0) -> dict:
    key = jax.random.key(seed)
    ks = jax.random.split(key, 16)
    f32 = jnp.float32
    D, F = D_MODEL, FFN_DIM
    ret_in = 2 * RET_HEADS * RET_KEY_DIM + 2 * RET_HEADS * RET_VAL_DIM
    ret_v = RET_HEADS * RET_VAL_DIM

    def nrm(k, shape, fan_in, scale=1.0):
        return jax.random.normal(k, shape, f32) * (fan_in ** -0.5) * scale

    return {
        "x": jax.random.normal(ks[0], (BATCH, SEQ, D), f32),
        "att_w_in": nrm(ks[1], (N_ATT_LAYERS, D, 3 * D), D),
        "att_w_out": nrm(ks[2], (N_ATT_LAYERS, D, D), D, DEEPNORM_BETA),
        "ret_w_in": nrm(ks[3], (N_RET_LAYERS, D, ret_in), D),
        "ret_w_out": nrm(ks[4], (N_RET_LAYERS, ret_v, D), ret_v, DEEPNORM_BETA),
        "ret_gn_g": 1.0 + 0.02 * jax.random.normal(ks[5], (N_RET_LAYERS, ret_v), f32),
        "ret_gn_b": 0.02 * jax.random.normal(ks[6], (N_RET_LAYERS, ret_v), f32),
        "mix_ln_g": 1.0 + 0.02 * jax.random.normal(ks[7], (DEPTH, D), f32),
        "mix_ln_b": 0.02 * jax.random.normal(ks[8], (DEPTH, D), f32),
        "ffn_ln_g": 1.0 + 0.02 * jax.random.normal(ks[9], (DEPTH, D), f32),
        "ffn_ln_b": 0.02 * jax.random.normal(ks[10], (DEPTH, D), f32),
        "ffn_w_up": nrm(ks[11], (DEPTH, D, 2 * F), D),
        "ffn_conv_w": nrm(ks[12], (DEPTH, CONV_WIDTH, 2 * F), CONV_WIDTH),
        "ffn_conv_b": 0.02 * jax.random.normal(ks[13], (DEPTH, 2 * F), f32),
        "ffn_w_down": nrm(ks[14], (DEPTH, F, D), F, DEEPNORM_BETA),
    }


def reference(x, att_w_in, att_w_out, ret_w_in, ret_w_out, ret_gn_g, ret_gn_b,
              mix_ln_g, mix_ln_b, ffn_ln_g, ffn_ln_b, ffn_w_up, ffn_conv_w,
              ffn_conv_b, ffn_w_down):
    for i in range(DEPTH):
        slot = i // N_MIXERS
        if i % N_MIXERS == 0:
            y = moba_attention(x, att_w_in[slot], att_w_out[slot])
        else:
            y = retention(x, ret_w_in[slot], ret_w_out[slot], ret_gn_g[slot], ret_gn_b[slot])
        x = layer_norm(DEEPNORM_ALPHA * x + y, mix_ln_g[i], mix_ln_b[i])
        f = conv_ffn(x, ffn_w_up[i], ffn_conv_w[i], ffn_conv_b[i], ffn_w_down[i])
        x = layer_norm(DEEPNORM_ALPHA * x + f, ffn_ln_g[i], ffn_ln_b[i])
    return x
```

```python
import functools
import math

import jax
import jax.numpy as jnp
from jax import lax
from jax.experimental import pallas as pl
from jax.experimental.pallas import tpu as pltpu

F32 = jnp.float32
BF16 = jnp.bfloat16

D_MODEL = 1024
DEPTH = 2
ATT_HEADS = 16
ATT_HEAD_DIM = 64
MOBA_BLOCK = 256
MOBA_TOPK = 3
RET_HEADS = 4
RET_KEY_DIM = 256
RET_VAL_DIM = 512
RET_CHUNK = 128
FFN_DIM = 2816
CONV_WIDTH = 3
LN_EPS = 1e-5
GN_EPS = 1e-5
DEEPNORM_ALPHA = (2.0 * DEPTH) ** 0.25
NEG_INF = -1e30

LANES = 128
HALO_ROWS = 16
VMEM_LIMIT = 56 * 1024 * 1024

_NT = (((1,), (1,)), ((), ()))
_TN = (((0,), (0,)), ((), ()))


def _layer_norm(z, g, b):
    mu = jnp.mean(z, axis=-1, keepdims=True)
    zc = z - mu
    var = jnp.mean(zc * zc, axis=-1, keepdims=True)
    return zc * lax.rsqrt(var + LN_EPS) * g + b


def _proj_kernel(x_ref, w_ref, o_ref):
    o_ref[...] = jnp.dot(x_ref[...].astype(BF16), w_ref[...],
                         preferred_element_type=F32).astype(o_ref.dtype)


def _proj(x, w, tm, tn, name):
    m, k = x.shape
    n = w.shape[1]
    return pl.pallas_call(
        _proj_kernel,
        grid=(m // tm, n // tn),
        in_specs=[pl.BlockSpec((tm, k), lambda i, j: (i, 0)),
                  pl.BlockSpec((k, tn), lambda i, j: (0, j))],
        out_specs=pl.BlockSpec((tm, tn), lambda i, j: (i, j)),
        out_shape=jax.ShapeDtypeStruct((m, n), BF16),
        compiler_params=pltpu.CompilerParams(
            dimension_semantics=("parallel", "parallel"),
            vmem_limit_bytes=VMEM_LIMIT),
        name=name,
    )(x, w)


def _outproj_ln_kernel(a_ref, w_ref, x_ref, g_ref, b_ref, o_ref):
    y = jnp.dot(a_ref[...], w_ref[...], preferred_element_type=F32)
    z = DEEPNORM_ALPHA * x_ref[...] + y
    o_ref[...] = _layer_norm(z, g_ref[...], b_ref[...])


def _outproj_ln(a, w, x, g, b, tm, name):
    m, k = a.shape
    d = w.shape[1]
    return pl.pallas_call(
        _outproj_ln_kernel,
        grid=(m // tm,),
        in_specs=[pl.BlockSpec((tm, k), lambda i: (i, 0)),
                  pl.BlockSpec((k, d), lambda i: (0, 0)),
                  pl.BlockSpec((tm, d), lambda i: (i, 0)),
                  pl.BlockSpec((1, d), lambda i: (0, 0)),
                  pl.BlockSpec((1, d), lambda i: (0, 0))],
        out_specs=pl.BlockSpec((tm, d), lambda i: (i, 0)),
        out_shape=jax.ShapeDtypeStruct((m, d), F32),
        compiler_params=pltpu.CompilerParams(
            dimension_semantics=("parallel",),
            vmem_limit_bytes=VMEM_LIMIT),
        name=name,
    )(a, w, x, g, b)


def _moba_kernel(q_ref, k_ref, v_ref, o_ref, km_ref, *, n_blk):
    blk = MOBA_BLOCK
    hp = pl.program_id(1)
    j = pl.program_id(2)

    @pl.when(j == 0)
    def _():
        for n in range(n_blk):
            kb = k_ref[n * blk:(n + 1) * blk, :].astype(F32)
            km_ref[n:n + 1, :] = jnp.sum(kb, axis=0, keepdims=True) * (1.0 / blk)

    q = q_ref[...]
    km = km_ref[...]
    km_hi = km.astype(BF16)
    km_lo = (km - km_hi.astype(F32)).astype(BF16)
    lane = lax.broadcasted_iota(jnp.int32, (1, LANES), 1)
    col = lax.broadcasted_iota(jnp.int32, (blk, n_blk), 1)
    rel = (lax.broadcasted_iota(jnp.int32, (blk, blk), 0)
           - lax.broadcasted_iota(jnp.int32, (blk, blk), 1)).astype(F32)
    k_own = k_ref[pl.ds(pl.multiple_of(j * blk, blk), blk), :]
    v_own = v_ref[pl.ds(pl.multiple_of(j * blk, blk), blk), :]

    outs = []
    for hh in range(2):
        head = 2 * hp + hh
        slope = jnp.exp2(jnp.full((1, 1), -0.5, F32) * (head + 1).astype(F32))
        in_head = (lane >= hh * ATT_HEAD_DIM) & (lane < (hh + 1) * ATT_HEAD_DIM)
        qh = jnp.where(in_head, q, jnp.zeros_like(q))

        gate = (lax.dot_general(qh, km_hi, _NT, preferred_element_type=F32)
                + lax.dot_general(qh, km_lo, _NT, preferred_element_type=F32))
        gate = jnp.where(col < j, gate, NEG_INF)
        sel = jnp.zeros((blk, n_blk), F32)
        for r in range(MOBA_TOPK):
            mx = jnp.max(gate, axis=1, keepdims=True)
            idx = jnp.min(jnp.where(gate == mx, col, n_blk), axis=1, keepdims=True)
            hit = col == idx
            sel = jnp.where(hit & (r < j), 1.0, sel)
            gate = jnp.where(hit, -3e38, gate)

        s = lax.dot_general(qh, k_own, _NT, preferred_element_type=F32)
        s = s - slope * rel
        s = jnp.where(rel >= 0, s, NEG_INF)
        m0 = jnp.max(s, axis=1, keepdims=True)
        p = jnp.exp(s - m0)
        l0 = jnp.sum(p, axis=1, keepdims=True)
        acc0 = jnp.dot(p.astype(BF16), v_own, preferred_element_type=F32)

        def body(n, carry, qh=qh, sel=sel, slope=slope):
            m, l, acc = carry
            row_sel = jnp.sum(jnp.where(col == n, sel, 0.0), axis=1, keepdims=True)
            start = pl.multiple_of(n * blk, blk)
            kn = k_ref[pl.ds(start, blk), :]
            vn = v_ref[pl.ds(start, blk), :]
            s = lax.dot_general(qh, kn, _NT, preferred_element_type=F32)
            s = s - slope * (rel + ((j - n) * blk).astype(F32))
            s = jnp.where(row_sel > 0.0, s, NEG_INF)
            m_new = jnp.maximum(m, jnp.max(s, axis=1, keepdims=True))
            a = jnp.exp(m - m_new)
            p = jnp.exp(s - m_new)
            l = a * l + jnp.sum(p, axis=1, keepdims=True)
            acc = a * acc + jnp.dot(p.astype(BF16), vn, preferred_element_type=F32)
            return m_new, l, acc

        _, l, acc = lax.fori_loop(0, j, body, (m0, l0, acc0))
        outs.append(acc / l)

    o_ref[...] = jnp.where(lane < ATT_HEAD_DIM, outs[0], outs[1]).astype(o_ref.dtype)


def _moba(qkv, batch, seq):
    blk = MOBA_BLOCK
    n_blk = seq // blk
    n_pairs = ATT_HEADS // 2
    return pl.pallas_call(
        functools.partial(_moba_kernel, n_blk=n_blk),
        grid=(batch, n_pairs, n_blk),
        in_specs=[pl.BlockSpec((blk, LANES), lambda b, h, j: (b * n_blk + j, h)),
                  pl.BlockSpec((seq, LANES), lambda b, h, j: (b, n_pairs + h)),
                  pl.BlockSpec((seq, LANES), lambda b, h, j: (b, 2 * n_pairs + h))],
        out_specs=pl.BlockSpec((blk, LANES), lambda b, h, j: (b * n_blk + j, h)),
        out_shape=jax.ShapeDtypeStruct((batch * seq, D_MODEL), BF16),
        scratch_shapes=[pltpu.VMEM((n_blk, LANES), F32)],
        compiler_params=pltpu.CompilerParams(
            dimension_semantics=("parallel", "parallel", "arbitrary"),
            vmem_limit_bytes=VMEM_LIMIT),
        name="moba_attention",
    )(qkv, qkv, qkv)


def _retention_kernel(q_ref, k_ref, v_ref, g_ref, din_ref, qd_ref, kd_ref, cd_ref,
                      gng_ref, gnb_ref, o_ref, state_ref, *, chunks):
    c_len = RET_CHUNK

    @pl.when(pl.program_id(2) == 0)
    def _():
        state_ref[...] = jnp.zeros_like(state_ref)

    decay_in = din_ref[0]
    q_decay = qd_ref[0]
    k_decay = kd_ref[0]
    chunk_decay = cd_ref[0]
    gn_g = gng_ref[...]
    gn_b = gnb_ref[...]

    def chunk(c, carry):
        rows = pl.ds(pl.multiple_of(c * c_len, c_len), c_len)
        q = q_ref[rows, :]
        k = k_ref[rows, :]
        v = v_ref[rows, :]
        state = state_ref[...]
        scores = lax.dot_general(q, k, _NT, preferred_element_type=F32) * decay_in
        inner = jnp.dot(scores.astype(BF16), v, preferred_element_type=F32)
        cross = jnp.dot(q, state.astype(BF16), preferred_element_type=F32) * q_decay
        kd = (k.astype(F32) * k_decay).astype(BF16)
        state_ref[...] = chunk_decay * state + lax.dot_general(
            kd, v, _TN, preferred_element_type=F32)
        o = inner + cross
        mu = jnp.mean(o, axis=-1, keepdims=True)
        oc = o - mu
        var = jnp.mean(oc * oc, axis=-1, keepdims=True)
        on = oc * lax.rsqrt(var + GN_EPS) * gn_g + gn_b
        gate = g_ref[rows, :].astype(F32)
        o_ref[rows, :] = (on * (gate * jax.nn.sigmoid(gate))).astype(o_ref.dtype)
        return carry

    lax.fori_loop(0, chunks, chunk, 0)


def _retention(proj, gn_g, gn_b, batch, seq, chunks_per_step):
    h_n, dk, dv, c_len = RET_HEADS, RET_KEY_DIM, RET_VAL_DIM, RET_CHUNK
    rows = chunks_per_step * c_len
    steps = seq // rows
    log_gamma = jnp.log1p(-jnp.exp2(-5.0 - jnp.arange(h_n, dtype=F32)))
    i = jnp.arange(c_len, dtype=F32)
    diff = i[:, None] - i[None, :]
    k_scale = dk ** -0.5
    decay_in = jnp.where(diff >= 0, jnp.exp(log_gamma[:, None, None] * jnp.maximum(diff, 0.0)), 0.0) * k_scale
    q_decay = jnp.exp(log_gamma[:, None] * (i + 1.0))[..., None]
    k_decay = jnp.exp(log_gamma[:, None] * (c_len - 1.0 - i))[..., None] * k_scale
    chunk_decay = jnp.exp(log_gamma * c_len)[:, None, None]
    k_off = h_n * dk // dk
    v_off = 2 * h_n * dk // dv
    g_off = v_off + h_n
    return pl.pallas_call(
        functools.partial(_retention_kernel, chunks=chunks_per_step),
        grid=(batch, h_n, steps),
        in_specs=[pl.BlockSpec((rows, dk), lambda b, h, s: (b * steps + s, h)),
                  pl.BlockSpec((rows, dk), lambda b, h, s: (b * steps + s, k_off + h)),
                  pl.BlockSpec((rows, dv), lambda b, h, s: (b * steps + s, v_off + h)),
                  pl.BlockSpec((rows, dv), lambda b, h, s: (b * steps + s, g_off + h)),
                  pl.BlockSpec((1, c_len, c_len), lambda b, h, s: (h, 0, 0)),
                  pl.BlockSpec((1, c_len, 1), lambda b, h, s: (h, 0, 0)),
                  pl.BlockSpec((1, c_len, 1), lambda b, h, s: (h, 0, 0)),
                  pl.BlockSpec((1, 1, 1), lambda b, h, s: (h, 0, 0)),
                  pl.BlockSpec((1, dv), lambda b, h, s: (0, h)),
                  pl.BlockSpec((1, dv), lambda b, h, s: (0, h))],
        out_specs=pl.BlockSpec((rows, dv), lambda b, h, s: (b * steps + s, h)),
        out_shape=jax.ShapeDtypeStruct((batch * seq, h_n * dv), BF16),
        scratch_shapes=[pltpu.VMEM((dk, dv), F32)],
        compiler_params=pltpu.CompilerParams(
            dimension_semantics=("parallel", "parallel", "arbitrary"),
            vmem_limit_bytes=VMEM_LIMIT),
        name="retention",
    )(proj, proj, proj, proj, decay_in, q_decay, k_decay, chunk_decay, gn_g, gn_b)


def _ffn_kernel(x_ref, halo_ref, wa_ref, wg_ref, cwa_ref, cwg_ref, cba_ref, cbg_ref,
                wd_ref, g_ref, b_ref, o_ref, xcat_ref, ua_ref, ug_ref, acc_ref,
                *, tm, tiles_per_seq):
    i = pl.program_id(0)
    f = pl.program_id(1)

    @pl.when(f == 0)
    def _():
        seq_start = (i % tiles_per_seq) == 0
        halo = jnp.where(seq_start, 0.0, halo_ref[...])
        xcat_ref[0:HALO_ROWS, :] = halo.astype(BF16)
        xcat_ref[HALO_ROWS:, :] = x_ref[...].astype(BF16)
        acc_ref[...] = jnp.zeros_like(acc_ref)

    xc = xcat_ref[...]
    ua_ref[...] = jnp.dot(xc, wa_ref[...], preferred_element_type=F32)
    ug_ref[...] = jnp.dot(xc, wg_ref[...], preferred_element_type=F32)

    def conv(u_ref, cw_ref, cb_ref):
        cw = cw_ref[...]
        h = cb_ref[...] + cw[2:3, :] * u_ref[HALO_ROWS:HALO_ROWS + tm, :]
        h = h + cw[0:1, :] * u_ref[HALO_ROWS - 2:HALO_ROWS - 2 + tm, :]
        h = h + cw[1:2, :] * u_ref[HALO_ROWS - 1:HALO_ROWS - 1 + tm, :]
        return h

    a = conv(ua_ref, cwa_ref, cba_ref)
    gate = conv(ug_ref, cwg_ref, cbg_ref)
    act = 0.5 * a * (1.0 + lax.erf(a * (1.0 / math.sqrt(2.0)))) * gate
    acc_ref[...] += jnp.dot(act.astype(BF16), wd_ref[...], preferred_element_type=F32)

    @pl.when(f == pl.num_programs(1) - 1)
    def _():
        z = DEEPNORM_ALPHA * x_ref[...] + acc_ref[...]
        o_ref[...] = _layer_norm(z, g_ref[...], b_ref[...])


def _ffn(x, w_up, conv_w, conv_b, w_down, g, b, seq, tm, tf, name):
    m, d = x.shape
    f_dim = w_down.shape[0]
    nf = f_dim // tf
    halo_blocks_per_tile = tm // HALO_ROWS
    return pl.pallas_call(
        functools.partial(_ffn_kernel, tm=tm, tiles_per_seq=seq // tm),
        grid=(m // tm, nf),
        in_specs=[pl.BlockSpec((tm, d), lambda i, f: (i, 0)),
                  pl.BlockSpec((HALO_ROWS, d),
                               lambda i, f: (jnp.maximum(i * halo_blocks_per_tile - 1, 0), 0)),
                  pl.BlockSpec((d, tf), lambda i, f: (0, f)),
                  pl.BlockSpec((d, tf), lambda i, f: (0, nf + f)),
                  pl.BlockSpec((CONV_WIDTH, tf), lambda i, f: (0, f)),
                  pl.BlockSpec((CONV_WIDTH, tf), lambda i, f: (0, nf + f)),
                  pl.BlockSpec((1, tf), lambda i, f: (0, f)),
                  pl.BlockSpec((1, tf), lambda i, f: (0, nf + f)),
                  pl.BlockSpec((tf, d), lambda i, f: (f, 0)),
                  pl.BlockSpec((1, d), lambda i, f: (0, 0)),
                  pl.BlockSpec((1, d), lambda i, f: (0, 0))],
        out_specs=pl.BlockSpec((tm, d), lambda i, f: (i, 0)),
        out_shape=jax.ShapeDtypeStruct((m, d), F32),
        scratch_shapes=[pltpu.VMEM((tm + HALO_ROWS, d), BF16),
                        pltpu.VMEM((tm + HALO_ROWS, tf), F32),
                        pltpu.VMEM((tm + HALO_ROWS, tf), F32),
                        pltpu.VMEM((tm, d), F32)],
        compiler_params=pltpu.CompilerParams(
            dimension_semantics=("parallel", "arbitrary"),
            vmem_limit_bytes=VMEM_LIMIT),
        name=name,
    )(x, x, w_up, w_up, conv_w, conv_w, conv_b, conv_b, w_down, g, b)


def kernel(x, att_w_in, att_w_out, ret_w_in, ret_w_out, ret_gn_g, ret_gn_b,
           mix_ln_g, mix_ln_b, ffn_ln_g, ffn_ln_b, ffn_w_up, ffn_conv_w,
           ffn_conv_b, ffn_w_down):
    batch, seq, d = x.shape
    h = x.reshape(batch * seq, d)

    def row(p):
        return p.reshape(1, -1)

    def ffn(h, i):
        return _ffn(h, ffn_w_up[i].astype(BF16), ffn_conv_w[i], row(ffn_conv_b[i]),
                    ffn_w_down[i].astype(BF16), row(ffn_ln_g[i]), row(ffn_ln_b[i]),
                    seq, tm=1024, tf=256, name=f"conv_ffn_{i}")

    q_scale = jnp.where(jnp.arange(3 * d) < d, ATT_HEAD_DIM ** -0.5, 1.0).astype(F32)
    w_qkv = (att_w_in[0] * q_scale).astype(BF16)
    qkv = _proj(h, w_qkv, tm=1024, tn=1024, name="att_in_proj")
    att = _moba(qkv, batch, seq)
    h = _outproj_ln(att, att_w_out[0].astype(BF16), h, row(mix_ln_g[0]), row(mix_ln_b[0]),
                    tm=512, name="att_out_proj_ln")
    h = ffn(h, 0)

    proj = _proj(h, ret_w_in[0].astype(BF16), tm=1024, tn=1024, name="ret_in_proj")
    ret = _retention(proj, row(ret_gn_g[0]), row(ret_gn_b[0]), batch, seq, chunks_per_step=8)
    h = _outproj_ln(ret, ret_w_out[0].astype(BF16), h, row(mix_ln_g[1]), row(mix_ln_b[1]),
                    tm=512, name="ret_out_proj_ln")
    h = ffn(h, 1)
    return h.reshape(batch, seq, d)
```

```python
import functools
import math

import jax
import jax.numpy as jnp
from jax import lax
from jax.experimental import pallas as pl
from jax.experimental.pallas import tpu as pltpu

F32 = jnp.float32
BF16 = jnp.bfloat16

D_MODEL = 1024
DEPTH = 2
ATT_HEADS = 16
ATT_HEAD_DIM = 64
MOBA_BLOCK = 256
MOBA_TOPK = 3
RET_HEADS = 4
RET_KEY_DIM = 256
RET_VAL_DIM = 512
RET_CHUNK = 128
FFN_DIM = 2816
CONV_WIDTH = 3
LN_EPS = 1e-5
GN_EPS = 1e-5
DEEPNORM_ALPHA = (2.0 * DEPTH) ** 0.25
NEG_INF = -1e30

LANES = 128
HALO_ROWS = 16
VMEM_LIMIT = 56 * 1024 * 1024

_NT = (((1,), (1,)), ((), ()))
_TN = (((0,), (0,)), ((), ()))


def _layer_norm(z, g, b):
    mu = jnp.mean(z, axis=-1, keepdims=True)
    zc = z - mu
    var = jnp.mean(zc * zc, axis=-1, keepdims=True)
    return zc * lax.rsqrt(var + LN_EPS) * g + b


def _proj_kernel(x_ref, w_ref, o_ref):
    o_ref[...] = jnp.dot(x_ref[...].astype(BF16), w_ref[...],
                         preferred_element_type=F32).astype(o_ref.dtype)


def _proj(x, w, tm, tn, name):
    m, k = x.shape
    n = w.shape[1]
    return pl.pallas_call(
        _proj_kernel,
        grid=(m // tm, n // tn),
        in_specs=[pl.BlockSpec((tm, k), lambda i, j: (i, 0)),
                  pl.BlockSpec((k, tn), lambda i, j: (0, j))],
        out_specs=pl.BlockSpec((tm, tn), lambda i, j: (i, j)),
        out_shape=jax.ShapeDtypeStruct((m, n), BF16),
        compiler_params=pltpu.CompilerParams(
            dimension_semantics=("parallel", "parallel"),
            vmem_limit_bytes=VMEM_LIMIT),
        name=name,
    )(x, w)


def _outproj_ln_kernel(a_ref, w_ref, x_ref, g_ref, b_ref, o_ref):
    y = jnp.dot(a_ref[...], w_ref[...], preferred_element_type=F32)
    z = DEEPNORM_ALPHA * x_ref[...] + y
    o_ref[...] = _layer_norm(z, g_ref[...], b_ref[...])


def _outproj_ln(a, w, x, g, b, tm, name):
    m, k = a.shape
    d = w.shape[1]
    return pl.pallas_call(
        _outproj_ln_kernel,
        grid=(m // tm,),
        in_specs=[pl.BlockSpec((tm, k), lambda i: (i, 0)),
                  pl.BlockSpec((k, d), lambda i: (0, 0)),
                  pl.BlockSpec((tm, d), lambda i: (i, 0)),
                  pl.BlockSpec((1, d), lambda i: (0, 0)),
                  pl.BlockSpec((1, d), lambda i: (0, 0))],
        out_specs=pl.BlockSpec((tm, d), lambda i: (i, 0)),
        out_shape=jax.ShapeDtypeStruct((m, d), F32),
        compiler_params=pltpu.CompilerParams(
            dimension_semantics=("parallel",),
            vmem_limit_bytes=VMEM_LIMIT),
        name=name,
    )(a, w, x, g, b)


_EXT_PARTS = 3
_SLOPE_ROW = _EXT_PARTS * 32
_GROUP = 4


def _split3(x):
    hi = x.astype(BF16).astype(F32)
    r1 = x - hi
    mid = r1.astype(BF16).astype(F32)
    lo = (r1 - mid).astype(BF16).astype(F32)
    return hi, mid, lo


def _moba_kernel(q_ref, k_ref, v_ref, o_ref, km_ref, kext_ref, lhs_ref,
                 m_ref, l_ref, acc_ref, *, n_blk):
    blk = MOBA_BLOCK
    rows = 2 * blk
    gkeys = _GROUP * blk
    hp = pl.program_id(1)
    j = pl.program_id(2)

    @pl.when(j == 0)
    def _():
        lane = lax.broadcasted_iota(jnp.int32, (blk, LANES), 1)
        key_off = lax.broadcasted_iota(jnp.int32, (blk, LANES), 0).astype(F32)
        is_onehot_lane = lane < _SLOPE_ROW
        is_off_lane = (lane >= _SLOPE_ROW) & (lane < _SLOPE_ROW + _EXT_PARTS)
        for n in range(n_blk):
            kb = k_ref[n * blk:(n + 1) * blk, :]
            kf = kb.astype(F32)
            km_ref[n:n + 1, :] = jnp.sum(kf, axis=0, keepdims=True) * (1.0 / blk)
            pat = jnp.where(is_onehot_lane & ((lane & (n_blk - 1)) == n), 1.0,
                            jnp.where(is_off_lane, key_off, 0.0))
            g, c = divmod(n, _GROUP)
            kext_ref[g, 0:LANES, c * blk:(c + 1) * blk] = kf.T.astype(BF16)
            kext_ref[g, LANES:2 * LANES, c * blk:(c + 1) * blk] = pat.T.astype(BF16)

    q = q_ref[...]
    lane1 = lax.broadcasted_iota(jnp.int32, (1, LANES), 1)
    zero = jnp.zeros_like(q)
    qs = jnp.concatenate([jnp.where(lane1 < ATT_HEAD_DIM, q, zero),
                          jnp.where(lane1 >= ATT_HEAD_DIM, q, zero)], axis=0)

    km = km_ref[...]
    km_hi = km.astype(BF16)
    km_lo = (km - km_hi.astype(F32)).astype(BF16)
    gate = (lax.dot_general(km_hi, qs, _NT, preferred_element_type=F32)
            + lax.dot_general(km_lo, qs, _NT, preferred_element_type=F32))
    blk_id = lax.broadcasted_iota(jnp.int32, (n_blk, rows), 0)
    qlane = lax.broadcasted_iota(jnp.int32, (1, rows), 1)
    gate = jnp.where(blk_id < j, gate, NEG_INF)
    sel = jnp.zeros((n_blk, rows), F32)
    for r in range(MOBA_TOPK):
        mx = jnp.max(gate, axis=0, keepdims=True)
        idx = jnp.min(jnp.where(gate == mx, blk_id, n_blk), axis=0, keepdims=True)
        hit = blk_id == idx
        sel = jnp.maximum(sel, jnp.where(hit, (r < j).astype(F32), 0.0))
        gate = jnp.where(hit, -3e38, gate)

    slope_a = jnp.exp2(jnp.full((1, 1), -0.5, F32) * (2 * hp + 1).astype(F32))
    slope_b = jnp.exp2(jnp.full((1, 1), -0.5, F32) * (2 * hp + 2).astype(F32))
    slope_q = jnp.where(qlane < blk, slope_a, slope_b)
    q_off = (qlane & (blk - 1)).astype(F32)
    bias = jnp.where(sel > 0.0, -(slope_q * (((j - blk_id) * blk).astype(F32) + q_off)), NEG_INF)
    slope_rows = jnp.concatenate(
        list(_split3(slope_q)) + [jnp.zeros((n_blk - _EXT_PARTS, rows), F32)], axis=0)
    ext_t = jnp.concatenate(list(_split3(bias)) + [slope_rows], axis=0)
    lhs_ref[:, 0:LANES] = qs
    lhs_ref[:, LANES:2 * LANES] = ext_t.T.astype(BF16)

    own = pl.multiple_of(j * blk, blk)
    q_pos = lax.broadcasted_iota(jnp.int32, (rows, blk), 0) & (blk - 1)
    rel = q_pos - lax.broadcasted_iota(jnp.int32, (rows, blk), 1)
    slope_col = jnp.where(lax.broadcasted_iota(jnp.int32, (rows, 1), 0) < blk, slope_a, slope_b)
    s = lax.dot_general(qs, k_ref[pl.ds(own, blk), :], _NT, preferred_element_type=F32)
    s = jnp.where(rel >= 0, s - slope_col * rel.astype(F32), NEG_INF)
    m0 = jnp.max(s, axis=1, keepdims=True)
    p = jnp.exp(s - m0)
    m_ref[...] = m0
    l_ref[...] = jnp.sum(p, axis=1, keepdims=True)
    acc_ref[...] = jnp.dot(p.astype(BF16), v_ref[pl.ds(own, blk), :],
                           preferred_element_type=F32)

    def body(g, carry):
        v_g = v_ref[pl.ds(pl.multiple_of(g * gkeys, gkeys), gkeys), :]
        for hh in range(2):
            r = slice(hh * blk, (hh + 1) * blk)
            s = jnp.dot(lhs_ref[r, :], kext_ref[g], preferred_element_type=F32)
            m_old = m_ref[r, :]
            m_new = jnp.maximum(m_old, jnp.max(s, axis=1, keepdims=True))
            a = jnp.exp(m_old - m_new)
            p = jnp.exp(s - m_new)
            l_ref[r, :] = a * l_ref[r, :] + jnp.sum(p, axis=1, keepdims=True)
            acc_ref[r, :] = a * acc_ref[r, :] + jnp.dot(p.astype(BF16), v_g,
                                                        preferred_element_type=F32)
            m_ref[r, :] = m_new
        return carry

    lax.fori_loop(0, (j + _GROUP - 1) // _GROUP, body, 0)
    out = acc_ref[...] / l_ref[...]
    o_ref[...] = jnp.where(lane1 < ATT_HEAD_DIM, out[0:blk], out[blk:rows]).astype(o_ref.dtype)


def _moba(qkv, batch, seq):
    blk = MOBA_BLOCK
    n_blk = seq // blk
    n_pairs = ATT_HEADS // 2
    assert n_blk == 32 and n_blk % _GROUP == 0
    return pl.pallas_call(
        functools.partial(_moba_kernel, n_blk=n_blk),
        grid=(batch, n_pairs, n_blk),
        in_specs=[pl.BlockSpec((blk, LANES), lambda b, h, j: (b * n_blk + j, h)),
                  pl.BlockSpec((seq, LANES), lambda b, h, j: (b, n_pairs + h)),
                  pl.BlockSpec((seq, LANES), lambda b, h, j: (b, 2 * n_pairs + h))],
        out_specs=pl.BlockSpec((blk, LANES), lambda b, h, j: (b * n_blk + j, h)),
        out_shape=jax.ShapeDtypeStruct((batch * seq, D_MODEL), BF16),
        scratch_shapes=[pltpu.VMEM((n_blk, LANES), F32),
                        pltpu.VMEM((n_blk // _GROUP, 2 * LANES, _GROUP * blk), BF16),
                        pltpu.VMEM((2 * blk, 2 * LANES), BF16),
                        pltpu.VMEM((2 * blk, 1), F32),
                        pltpu.VMEM((2 * blk, 1), F32),
                        pltpu.VMEM((2 * blk, LANES), F32)],
        compiler_params=pltpu.CompilerParams(
            dimension_semantics=("parallel", "parallel", "arbitrary"),
            vmem_limit_bytes=VMEM_LIMIT),
        name="moba_attention",
    )(qkv, qkv, qkv)


def _retention_kernel(q_ref, k_ref, v_ref, g_ref, din_ref, qd_ref, kd_ref, cd_ref,
                      gng_ref, gnb_ref, o_ref, state_ref, *, chunks):
    c_len = RET_CHUNK

    @pl.when(pl.program_id(2) == 0)
    def _():
        state_ref[...] = jnp.zeros_like(state_ref)

    decay_in = din_ref[0]
    q_decay = qd_ref[0]
    k_decay = kd_ref[0]
    chunk_decay = cd_ref[0]
    gn_g = gng_ref[...]
    gn_b = gnb_ref[...]

    def chunk(c, carry):
        rows = pl.ds(pl.multiple_of(c * c_len, c_len), c_len)
        q = q_ref[rows, :]
        k = k_ref[rows, :]
        v = v_ref[rows, :]
        state = state_ref[...]
        scores = lax.dot_general(q, k, _NT, preferred_element_type=F32) * decay_in
        inner = jnp.dot(scores.astype(BF16), v, preferred_element_type=F32)
        cross = jnp.dot(q, state.astype(BF16), preferred_element_type=F32) * q_decay
        kd = (k.astype(F32) * k_decay).astype(BF16)
        state_ref[...] = chunk_decay * state + lax.dot_general(
            kd, v, _TN, preferred_element_type=F32)
        o = inner + cross
        mu = jnp.mean(o, axis=-1, keepdims=True)
        oc = o - mu
        var = jnp.mean(oc * oc, axis=-1, keepdims=True)
        on = oc * lax.rsqrt(var + GN_EPS) * gn_g + gn_b
        gate = g_ref[rows, :].astype(F32)
        o_ref[rows, :] = (on * (gate * jax.nn.sigmoid(gate))).astype(o_ref.dtype)
        return carry

    lax.fori_loop(0, chunks, chunk, 0, unroll=True)


def _retention(proj, gn_g, gn_b, batch, seq, chunks_per_step):
    h_n, dk, dv, c_len = RET_HEADS, RET_KEY_DIM, RET_VAL_DIM, RET_CHUNK
    rows = chunks_per_step * c_len
    steps = seq // rows
    log_gamma = jnp.log1p(-jnp.exp2(-5.0 - jnp.arange(h_n, dtype=F32)))
    i = jnp.arange(c_len, dtype=F32)
    diff = i[:, None] - i[None, :]
    k_scale = dk ** -0.5
    decay_in = jnp.where(diff >= 0, jnp.exp(log_gamma[:, None, None] * jnp.maximum(diff, 0.0)), 0.0) * k_scale
    q_decay = jnp.exp(log_gamma[:, None] * (i + 1.0))[..., None]
    k_decay = jnp.exp(log_gamma[:, None] * (c_len - 1.0 - i))[..., None] * k_scale
    chunk_decay = jnp.exp(log_gamma * c_len)[:, None, None]
    k_off = h_n * dk // dk
    v_off = 2 * h_n * dk // dv
    g_off = v_off + h_n
    return pl.pallas_call(
        functools.partial(_retention_kernel, chunks=chunks_per_step),
        grid=(batch, h_n, steps),
        in_specs=[pl.BlockSpec((rows, dk), lambda b, h, s: (b * steps + s, h)),
                  pl.BlockSpec((rows, dk), lambda b, h, s: (b * steps + s, k_off + h)),
                  pl.BlockSpec((rows, dv), lambda b, h, s: (b * steps + s, v_off + h)),
                  pl.BlockSpec((rows, dv), lambda b, h, s: (b * steps + s, g_off + h)),
                  pl.BlockSpec((1, c_len, c_len), lambda b, h, s: (h, 0, 0)),
                  pl.BlockSpec((1, c_len, 1), lambda b, h, s: (h, 0, 0)),
                  pl.BlockSpec((1, c_len, 1), lambda b, h, s: (h, 0, 0)),
                  pl.BlockSpec((1, 1, 1), lambda b, h, s: (h, 0, 0)),
                  pl.BlockSpec((1, dv), lambda b, h, s: (0, h)),
                  pl.BlockSpec((1, dv), lambda b, h, s: (0, h))],
        out_specs=pl.BlockSpec((rows, dv), lambda b, h, s: (b * steps + s, h)),
        out_shape=jax.ShapeDtypeStruct((batch * seq, h_n * dv), BF16),
        scratch_shapes=[pltpu.VMEM((dk, dv), F32)],
        compiler_params=pltpu.CompilerParams(
            dimension_semantics=("parallel", "parallel", "arbitrary"),
            vmem_limit_bytes=VMEM_LIMIT),
        name="retention",
    )(proj, proj, proj, proj, decay_in, q_decay, k_decay, chunk_decay, gn_g, gn_b)


def _ffn_kernel(x_ref, halo_ref, wup_ref, cw_ref, cb_ref, wd_ref, g_ref, b_ref, o_ref,
                xcat_ref, u_ref, *, tm, tf, f_dim, tiles_per_seq):
    seq_start = (pl.program_id(0) % tiles_per_seq) == 0
    halo = jnp.where(seq_start, 0.0, halo_ref[...])
    xcat_ref[0:HALO_ROWS, :] = halo.astype(BF16)
    xcat_ref[HALO_ROWS:, :] = x_ref[...].astype(BF16)
    n_chunks = f_dim // tf

    def up(c):
        xc = xcat_ref[...]
        slot = c % 2
        for half in range(2):
            cols = slice(half * f_dim + c * tf, half * f_dim + (c + 1) * tf)
            u_ref[slot, half] = jnp.dot(xc, wup_ref[:, cols], preferred_element_type=F32)

    def conv(c, half):
        cols = slice(half * f_dim + c * tf, half * f_dim + (c + 1) * tf)
        cw = cw_ref[:, cols]
        u = u_ref.at[c % 2, half]
        h = cb_ref[:, cols] + cw[2:3, :] * u[HALO_ROWS:HALO_ROWS + tm, :]
        h = h + cw[0:1, :] * u[HALO_ROWS - 2:HALO_ROWS - 2 + tm, :]
        h = h + cw[1:2, :] * u[HALO_ROWS - 1:HALO_ROWS - 1 + tm, :]
        return h

    up(0)
    acc = None
    for c in range(n_chunks):
        if c + 1 < n_chunks:
            up(c + 1)
        a = conv(c, 0)
        gate = conv(c, 1)
        act = 0.5 * a * (1.0 + lax.erf(a * (1.0 / math.sqrt(2.0)))) * gate
        d = jnp.dot(act.astype(BF16), wd_ref[c * tf:(c + 1) * tf, :], preferred_element_type=F32)
        acc = d if acc is None else acc + d

    z = DEEPNORM_ALPHA * x_ref[...] + acc
    o_ref[...] = _layer_norm(z, g_ref[...], b_ref[...])


def _ffn(x, w_up, conv_w, conv_b, w_down, g, b, seq, tm, tf, name):
    m, d = x.shape
    f_dim = w_down.shape[0]
    halo_blocks_per_tile = tm // HALO_ROWS
    resident = pl.Buffered(1)
    return pl.pallas_call(
        functools.partial(_ffn_kernel, tm=tm, tf=tf, f_dim=f_dim, tiles_per_seq=seq // tm),
        grid=(m // tm,),
        in_specs=[pl.BlockSpec((tm, d), lambda i: (i, 0)),
                  pl.BlockSpec((HALO_ROWS, d),
                               lambda i: (jnp.maximum(i * halo_blocks_per_tile - 1, 0), 0)),
                  pl.BlockSpec((d, 2 * f_dim), lambda i: (0, 0), pipeline_mode=resident),
                  pl.BlockSpec((CONV_WIDTH, 2 * f_dim), lambda i: (0, 0), pipeline_mode=resident),
                  pl.BlockSpec((1, 2 * f_dim), lambda i: (0, 0), pipeline_mode=resident),
                  pl.BlockSpec((f_dim, d), lambda i: (0, 0), pipeline_mode=resident),
                  pl.BlockSpec((1, d), lambda i: (0, 0)),
                  pl.BlockSpec((1, d), lambda i: (0, 0))],
        out_specs=pl.BlockSpec((tm, d), lambda i: (i, 0)),
        out_shape=jax.ShapeDtypeStruct((m, d), F32),
        scratch_shapes=[pltpu.VMEM((tm + HALO_ROWS, d), BF16),
                        pltpu.VMEM((2, 2, tm + HALO_ROWS, tf), F32)],
        compiler_params=pltpu.CompilerParams(
            dimension_semantics=("parallel",),
            vmem_limit_bytes=VMEM_LIMIT),
        name=name,
    )(x, x, w_up, conv_w, conv_b, w_down, g, b)


def kernel(x, att_w_in, att_w_out, ret_w_in, ret_w_out, ret_gn_g, ret_gn_b,
           mix_ln_g, mix_ln_b, ffn_ln_g, ffn_ln_b, ffn_w_up, ffn_conv_w,
           ffn_conv_b, ffn_w_down):
    batch, seq, d = x.shape
    h = x.reshape(batch * seq, d)

    def row(p):
        return p.reshape(1, -1)

    def ffn(h, i):
        return _ffn(h, ffn_w_up[i].astype(BF16), ffn_conv_w[i], row(ffn_conv_b[i]),
                    ffn_w_down[i].astype(BF16), row(ffn_ln_g[i]), row(ffn_ln_b[i]),
                    seq, tm=512, tf=256, name=f"conv_ffn_{i}")

    q_scale = jnp.where(jnp.arange(3 * d) < d, ATT_HEAD_DIM ** -0.5, 1.0).astype(F32)
    w_qkv = (att_w_in[0] * q_scale).astype(BF16)
    qkv = _proj(h, w_qkv, tm=1024, tn=1024, name="att_in_proj")
    att = _moba(qkv, batch, seq)
    h = _outproj_ln(att, att_w_out[0].astype(BF16), h, row(mix_ln_g[0]), row(mix_ln_b[0]),
                    tm=512, name="att_out_proj_ln")
    h = ffn(h, 0)

    proj = _proj(h, ret_w_in[0].astype(BF16), tm=1024, tn=1024, name="ret_in_proj")
    ret = _retention(proj, row(ret_gn_g[0]), row(ret_gn_b[0]), batch, seq, chunks_per_step=8)
    h = _outproj_ln(ret, ret_w_out[0].astype(BF16), h, row(mix_ln_g[1]), row(mix_ln_b[1]),
                    tm=512, name="ret_out_proj_ln")
    h = ffn(h, 1)
    return h.reshape(batch, seq, d)
```

```python
import functools
import math

import jax
import jax.numpy as jnp
from jax import lax
from jax.experimental import pallas as pl
from jax.experimental.pallas import tpu as pltpu

F32 = jnp.float32
BF16 = jnp.bfloat16

D_MODEL = 1024
DEPTH = 2
ATT_HEADS = 16
ATT_HEAD_DIM = 64
MOBA_BLOCK = 256
MOBA_TOPK = 3
RET_HEADS = 4
RET_KEY_DIM = 256
RET_VAL_DIM = 512
RET_CHUNK = 128
FFN_DIM = 2816
CONV_WIDTH = 3
LN_EPS = 1e-5
GN_EPS = 1e-5
DEEPNORM_ALPHA = (2.0 * DEPTH) ** 0.25
NEG_INF = -1e30

LANES = 128
HALO_ROWS = 16
VMEM_LIMIT = 56 * 1024 * 1024

_NT = (((1,), (1,)), ((), ()))
_TN = (((0,), (0,)), ((), ()))


def _layer_norm(z, g, b):
    mu = jnp.mean(z, axis=-1, keepdims=True)
    zc = z - mu
    var = jnp.mean(zc * zc, axis=-1, keepdims=True)
    return zc * lax.rsqrt(var + LN_EPS) * g + b


def _proj_kernel(x_ref, w_ref, o_ref):
    o_ref[...] = jnp.dot(x_ref[...].astype(BF16), w_ref[...],
                         preferred_element_type=F32).astype(o_ref.dtype)


def _proj(x, w, tm, tn, name):
    m, k = x.shape
    n = w.shape[1]
    return pl.pallas_call(
        _proj_kernel,
        grid=(m // tm, n // tn),
        in_specs=[pl.BlockSpec((tm, k), lambda i, j: (i, 0)),
                  pl.BlockSpec((k, tn), lambda i, j: (0, j))],
        out_specs=pl.BlockSpec((tm, tn), lambda i, j: (i, j)),
        out_shape=jax.ShapeDtypeStruct((m, n), BF16),
        compiler_params=pltpu.CompilerParams(
            dimension_semantics=("parallel", "parallel"),
            vmem_limit_bytes=VMEM_LIMIT),
        name=name,
    )(x, w)


def _outproj_ln_kernel(a_ref, w_ref, x_ref, g_ref, b_ref, o_ref):
    y = jnp.dot(a_ref[...], w_ref[...], preferred_element_type=F32)
    z = DEEPNORM_ALPHA * x_ref[...] + y
    o_ref[...] = _layer_norm(z, g_ref[...], b_ref[...])


def _outproj_ln(a, w, x, g, b, tm, name):
    m, k = a.shape
    d = w.shape[1]
    return pl.pallas_call(
        _outproj_ln_kernel,
        grid=(m // tm,),
        in_specs=[pl.BlockSpec((tm, k), lambda i: (i, 0)),
                  pl.BlockSpec((k, d), lambda i: (0, 0)),
                  pl.BlockSpec((tm, d), lambda i: (i, 0)),
                  pl.BlockSpec((1, d), lambda i: (0, 0)),
                  pl.BlockSpec((1, d), lambda i: (0, 0))],
        out_specs=pl.BlockSpec((tm, d), lambda i: (i, 0)),
        out_shape=jax.ShapeDtypeStruct((m, d), F32),
        compiler_params=pltpu.CompilerParams(
            dimension_semantics=("parallel",),
            vmem_limit_bytes=VMEM_LIMIT),
        name=name,
    )(a, w, x, g, b)


_EXT_PARTS = 3
_SLOPE_ROW = _EXT_PARTS * 32
_GROUP = 2


def _split3(x):
    hi = x.astype(BF16).astype(F32)
    r1 = x - hi
    mid = r1.astype(BF16).astype(F32)
    lo = (r1 - mid).astype(BF16).astype(F32)
    return hi, mid, lo


def _moba_kernel(q_ref, k_ref, v_ref, o_ref, km_ref, kext_ref, vt_ref, lhst_ref,
                 sa_ref, sb_ref, mga_ref, mgb_ref, m_ref, l_ref, acc_ref, *, n_blk):
    blk = MOBA_BLOCK
    hd = ATT_HEAD_DIM
    nq = 2 * blk
    gkeys = _GROUP * blk
    hp = pl.program_id(1)
    j = pl.program_id(2)

    @pl.when(j == 0)
    def _():
        lane = lax.broadcasted_iota(jnp.int32, (blk, LANES), 1)
        key_off = lax.broadcasted_iota(jnp.int32, (blk, LANES), 0).astype(F32)
        is_onehot_lane = lane < _SLOPE_ROW
        is_off_lane = (lane >= _SLOPE_ROW) & (lane < _SLOPE_ROW + _EXT_PARTS)
        for n in range(n_blk):
            rows = slice(n * blk, (n + 1) * blk)
            kb = k_ref[rows, :]
            km_ref[n:n + 1, :] = jnp.sum(kb.astype(F32), axis=0, keepdims=True) * (1.0 / blk)
            pat = jnp.where(is_onehot_lane & ((lane & (n_blk - 1)) == n), 1.0,
                            jnp.where(is_off_lane, key_off, 0.0))
            kext_ref[rows, 0:LANES] = kb
            kext_ref[rows, LANES:2 * LANES] = pat.astype(BF16)
            g, c = divmod(n, _GROUP)
            vt_ref[g, :, c * blk:(c + 1) * blk] = v_ref[rows, :].astype(F32).T.astype(BF16)

    q_t = q_ref[...].astype(F32).T
    feat = lax.broadcasted_iota(jnp.int32, (LANES, blk), 0)
    qs_t = jnp.concatenate([jnp.where(feat < hd, q_t, 0.0),
                            jnp.where(feat >= hd, q_t, 0.0)], axis=1).astype(BF16)

    km = km_ref[...]
    km_hi = km.astype(BF16)
    km_lo = (km - km_hi.astype(F32)).astype(BF16)
    gate = (jnp.dot(km_hi, qs_t, preferred_element_type=F32)
            + jnp.dot(km_lo, qs_t, preferred_element_type=F32))
    blk_id = lax.broadcasted_iota(jnp.int32, (n_blk, nq), 0)
    qlane = lax.broadcasted_iota(jnp.int32, (1, nq), 1)
    gate = jnp.where(blk_id < j, gate, NEG_INF)
    sel = jnp.zeros((n_blk, nq), F32)
    for r in range(MOBA_TOPK):
        mx = jnp.max(gate, axis=0, keepdims=True)
        idx = jnp.min(jnp.where(gate == mx, blk_id, n_blk), axis=0, keepdims=True)
        hit = blk_id == idx
        sel = jnp.maximum(sel, jnp.where(hit, (r < j).astype(F32), 0.0))
        gate = jnp.where(hit, -3e38, gate)

    slope_a = jnp.exp2(jnp.full((1, 1), -0.5, F32) * (2 * hp + 1).astype(F32))
    slope_b = jnp.exp2(jnp.full((1, 1), -0.5, F32) * (2 * hp + 2).astype(F32))
    slope_q = jnp.where(qlane < blk, slope_a, slope_b)
    q_off = (qlane & (blk - 1)).astype(F32)
    bias = jnp.where(sel > 0.0, -(slope_q * (((j - blk_id) * blk).astype(F32) + q_off)), NEG_INF)
    slope_rows = jnp.concatenate(
        list(_split3(slope_q)) + [jnp.zeros((n_blk - _EXT_PARTS, nq), F32)], axis=0)
    ext_t = jnp.concatenate(list(_split3(bias)) + [slope_rows], axis=0)
    lhst_ref[0:LANES, :] = qs_t
    lhst_ref[LANES:2 * LANES, :] = ext_t.astype(BF16)

    own = pl.multiple_of(j * blk, blk)
    k_own = k_ref[pl.ds(own, blk), :]
    vt_own = v_ref[pl.ds(own, blk), :].astype(F32).T.astype(BF16)
    rel = (lax.broadcasted_iota(jnp.int32, (blk, blk), 1)
           - lax.broadcasted_iota(jnp.int32, (blk, blk), 0))
    rel_f = rel.astype(F32)
    for hh, slope_h in enumerate((slope_a, slope_b)):
        s = jnp.dot(k_own, qs_t[:, hh * blk:(hh + 1) * blk], preferred_element_type=F32)
        s = jnp.where(rel >= 0, s - slope_h * rel_f, NEG_INF)
        m = jnp.max(s, axis=0, keepdims=True)
        p = jnp.exp(s - m)
        m_ref[hh] = m
        l_ref[hh] = jnp.sum(p, axis=0, keepdims=True)
        acc_ref[hh] = jnp.dot(vt_own[hh * hd:(hh + 1) * hd, :], p.astype(BF16),
                              preferred_element_type=F32)

    def scores(g, s_ref, mg_ref):
        keys = pl.ds(pl.multiple_of(g * gkeys, gkeys), gkeys)
        for hh in range(2):
            s = jnp.dot(kext_ref[keys, :], lhst_ref[:, hh * blk:(hh + 1) * blk],
                        preferred_element_type=F32)
            s_ref[hh] = s
            mg_ref[hh] = jnp.max(s, axis=0, keepdims=True)

    def accumulate(g, s_ref, mg_ref):
        for hh in range(2):
            m_old = m_ref[hh]
            m_new = jnp.maximum(m_old, mg_ref[hh])
            a = jnp.exp(m_old - m_new)
            p = jnp.exp(s_ref[hh] - m_new)
            l_ref[hh] = a * l_ref[hh] + jnp.sum(p, axis=0, keepdims=True)
            acc_ref[hh] = a * acc_ref[hh] + jnp.dot(
                vt_ref[g, hh * hd:(hh + 1) * hd, :], p.astype(BF16), preferred_element_type=F32)
            m_ref[hh] = m_new

    n_iter = (j + 2 * _GROUP - 1) // (2 * _GROUP)
    last_tile = n_blk // _GROUP - 1

    @pl.when(n_iter > 0)
    def _():
        scores(0, sa_ref, mga_ref)

    def body(t, carry):
        scores(2 * t + 1, sb_ref, mgb_ref)
        accumulate(2 * t, sa_ref, mga_ref)
        scores(jnp.minimum(2 * t + 2, last_tile), sa_ref, mga_ref)
        accumulate(2 * t + 1, sb_ref, mgb_ref)
        return carry

    lax.fori_loop(0, n_iter, body, 0)
    out_t = jnp.concatenate([acc_ref[0] / l_ref[0], acc_ref[1] / l_ref[1]], axis=0)
    o_ref[...] = out_t.T.astype(o_ref.dtype)


def _moba(qkv, batch, seq):
    blk = MOBA_BLOCK
    n_blk = seq // blk
    n_pairs = ATT_HEADS // 2
    assert n_blk == 32 and n_blk % (2 * _GROUP) == 0
    return pl.pallas_call(
        functools.partial(_moba_kernel, n_blk=n_blk),
        grid=(batch, n_pairs, n_blk),
        in_specs=[pl.BlockSpec((blk, LANES), lambda b, h, j: (b * n_blk + j, h)),
                  pl.BlockSpec((seq, LANES), lambda b, h, j: (b, n_pairs + h)),
                  pl.BlockSpec((seq, LANES), lambda b, h, j: (b, 2 * n_pairs + h))],
        out_specs=pl.BlockSpec((blk, LANES), lambda b, h, j: (b * n_blk + j, h)),
        out_shape=jax.ShapeDtypeStruct((batch * seq, D_MODEL), BF16),
        scratch_shapes=[pltpu.VMEM((n_blk, LANES), F32),
                        pltpu.VMEM((seq, 2 * LANES), BF16),
                        pltpu.VMEM((n_blk // _GROUP, LANES, _GROUP * blk), BF16),
                        pltpu.VMEM((2 * LANES, 2 * blk), BF16),
                        pltpu.VMEM((2, _GROUP * blk, blk), F32),
                        pltpu.VMEM((2, _GROUP * blk, blk), F32),
                        pltpu.VMEM((2, 1, blk), F32),
                        pltpu.VMEM((2, 1, blk), F32),
                        pltpu.VMEM((2, 1, blk), F32),
                        pltpu.VMEM((2, 1, blk), F32),
                        pltpu.VMEM((2, ATT_HEAD_DIM, blk), F32)],
        compiler_params=pltpu.CompilerParams(
            dimension_semantics=("parallel", "parallel", "arbitrary"),
            vmem_limit_bytes=VMEM_LIMIT),
        name="moba_attention",
    )(qkv, qkv, qkv)


def _retention_kernel(q_ref, k_ref, v_ref, g_ref, din_ref, qd_ref, kd_ref, cd_ref,
                      gng_ref, gnb_ref, o_ref, state_ref, *, chunks):
    c_len = RET_CHUNK

    @pl.when(pl.program_id(2) == 0)
    def _():
        state_ref[...] = jnp.zeros_like(state_ref)

    decay_in = din_ref[0]
    q_decay = qd_ref[0]
    k_decay = kd_ref[0]
    chunk_decay = cd_ref[0]
    gn_g = gng_ref[...]
    gn_b = gnb_ref[...]

    def chunk(c, carry):
        rows = pl.ds(pl.multiple_of(c * c_len, c_len), c_len)
        q = q_ref[rows, :]
        k = k_ref[rows, :]
        v = v_ref[rows, :]
        state = state_ref[...]
        scores = lax.dot_general(q, k, _NT, preferred_element_type=F32) * decay_in
        inner = jnp.dot(scores.astype(BF16), v, preferred_element_type=F32)
        cross = jnp.dot(q, state.astype(BF16), preferred_element_type=F32) * q_decay
        kd = (k.astype(F32) * k_decay).astype(BF16)
        state_ref[...] = chunk_decay * state + lax.dot_general(
            kd, v, _TN, preferred_element_type=F32)
        o = inner + cross
        mu = jnp.mean(o, axis=-1, keepdims=True)
        oc = o - mu
        var = jnp.mean(oc * oc, axis=-1, keepdims=True)
        on = oc * lax.rsqrt(var + GN_EPS) * gn_g + gn_b
        gate = g_ref[rows, :].astype(F32)
        o_ref[rows, :] = (on * (gate * jax.nn.sigmoid(gate))).astype(o_ref.dtype)
        return carry

    lax.fori_loop(0, chunks, chunk, 0, unroll=True)


def _retention(proj, gn_g, gn_b, batch, seq, chunks_per_step):
    h_n, dk, dv, c_len = RET_HEADS, RET_KEY_DIM, RET_VAL_DIM, RET_CHUNK
    rows = chunks_per_step * c_len
    steps = seq // rows
    log_gamma = jnp.log1p(-jnp.exp2(-5.0 - jnp.arange(h_n, dtype=F32)))
    i = jnp.arange(c_len, dtype=F32)
    diff = i[:, None] - i[None, :]
    k_scale = dk ** -0.5
    decay_in = jnp.where(diff >= 0, jnp.exp(log_gamma[:, None, None] * jnp.maximum(diff, 0.0)), 0.0) * k_scale
    q_decay = jnp.exp(log_gamma[:, None] * (i + 1.0))[..., None]
    k_decay = jnp.exp(log_gamma[:, None] * (c_len - 1.0 - i))[..., None] * k_scale
    chunk_decay = jnp.exp(log_gamma * c_len)[:, None, None]
    k_off = h_n * dk // dk
    v_off = 2 * h_n * dk // dv
    g_off = v_off + h_n
    return pl.pallas_call(
        functools.partial(_retention_kernel, chunks=chunks_per_step),
        grid=(batch, h_n, steps),
        in_specs=[pl.BlockSpec((rows, dk), lambda b, h, s: (b * steps + s, h)),
                  pl.BlockSpec((rows, dk), lambda b, h, s: (b * steps + s, k_off + h)),
                  pl.BlockSpec((rows, dv), lambda b, h, s: (b * steps + s, v_off + h)),
                  pl.BlockSpec((rows, dv), lambda b, h, s: (b * steps + s, g_off + h)),
                  pl.BlockSpec((1, c_len, c_len), lambda b, h, s: (h, 0, 0)),
                  pl.BlockSpec((1, c_len, 1), lambda b, h, s: (h, 0, 0)),
                  pl.BlockSpec((1, c_len, 1), lambda b, h, s: (h, 0, 0)),
                  pl.BlockSpec((1, 1, 1), lambda b, h, s: (h, 0, 0)),
                  pl.BlockSpec((1, dv), lambda b, h, s: (0, h)),
                  pl.BlockSpec((1, dv), lambda b, h, s: (0, h))],
        out_specs=pl.BlockSpec((rows, dv), lambda b, h, s: (b * steps + s, h)),
        out_shape=jax.ShapeDtypeStruct((batch * seq, h_n * dv), BF16),
        scratch_shapes=[pltpu.VMEM((dk, dv), F32)],
        compiler_params=pltpu.CompilerParams(
            dimension_semantics=("parallel", "parallel", "arbitrary"),
            vmem_limit_bytes=VMEM_LIMIT),
        name="retention",
    )(proj, proj, proj, proj, decay_in, q_decay, k_decay, chunk_decay, gn_g, gn_b)


def _ffn_kernel(x_ref, halo_ref, wup_ref, cw_ref, cb_ref, wd_ref, g_ref, b_ref, o_ref,
                xcat_ref, u_ref, *, tm, tf, f_dim, tiles_per_seq):
    seq_start = (pl.program_id(0) % tiles_per_seq) == 0
    halo = jnp.where(seq_start, 0.0, halo_ref[...])
    xcat_ref[0:HALO_ROWS, :] = halo.astype(BF16)
    xcat_ref[HALO_ROWS:, :] = x_ref[...].astype(BF16)
    n_chunks = f_dim // tf

    def up(c):
        xc = xcat_ref[...]
        slot = c % 2
        for half in range(2):
            cols = slice(half * f_dim + c * tf, half * f_dim + (c + 1) * tf)
            u_ref[slot, half] = jnp.dot(xc, wup_ref[:, cols], preferred_element_type=F32)

    def conv(c, half):
        cols = slice(half * f_dim + c * tf, half * f_dim + (c + 1) * tf)
        cw = cw_ref[:, cols]
        u = u_ref.at[c % 2, half]
        h = cb_ref[:, cols] + cw[2:3, :] * u[HALO_ROWS:HALO_ROWS + tm, :]
        h = h + cw[0:1, :] * u[HALO_ROWS - 2:HALO_ROWS - 2 + tm, :]
        h = h + cw[1:2, :] * u[HALO_ROWS - 1:HALO_ROWS - 1 + tm, :]
        return h

    up(0)
    acc = None
    for c in range(n_chunks):
        if c + 1 < n_chunks:
            up(c + 1)
        a = conv(c, 0)
        gate = conv(c, 1)
        act = 0.5 * a * (1.0 + lax.erf(a * (1.0 / math.sqrt(2.0)))) * gate
        d = jnp.dot(act.astype(BF16), wd_ref[c * tf:(c + 1) * tf, :], preferred_element_type=F32)
        acc = d if acc is None else acc + d

    z = DEEPNORM_ALPHA * x_ref[...] + acc
    o_ref[...] = _layer_norm(z, g_ref[...], b_ref[...])


def _ffn(x, w_up, conv_w, conv_b, w_down, g, b, seq, tm, tf, name):
    m, d = x.shape
    f_dim = w_down.shape[0]
    halo_blocks_per_tile = tm // HALO_ROWS
    resident = pl.Buffered(1)
    return pl.pallas_call(
        functools.partial(_ffn_kernel, tm=tm, tf=tf, f_dim=f_dim, tiles_per_seq=seq // tm),
        grid=(m // tm,),
        in_specs=[pl.BlockSpec((tm, d), lambda i: (i, 0)),
                  pl.BlockSpec((HALO_ROWS, d),
                               lambda i: (jnp.maximum(i * halo_blocks_per_tile - 1, 0), 0)),
                  pl.BlockSpec((d, 2 * f_dim), lambda i: (0, 0), pipeline_mode=resident),
                  pl.BlockSpec((CONV_WIDTH, 2 * f_dim), lambda i: (0, 0), pipeline_mode=resident),
                  pl.BlockSpec((1, 2 * f_dim), lambda i: (0, 0), pipeline_mode=resident),
                  pl.BlockSpec((f_dim, d), lambda i: (0, 0), pipeline_mode=resident),
                  pl.BlockSpec((1, d), lambda i: (0, 0)),
                  pl.BlockSpec((1, d), lambda i: (0, 0))],
        out_specs=pl.BlockSpec((tm, d), lambda i: (i, 0)),
        out_shape=jax.ShapeDtypeStruct((m, d), F32),
        scratch_shapes=[pltpu.VMEM((tm + HALO_ROWS, d), BF16),
                        pltpu.VMEM((2, 2, tm + HALO_ROWS, tf), F32)],
        compiler_params=pltpu.CompilerParams(
            dimension_semantics=("parallel",),
            vmem_limit_bytes=VMEM_LIMIT),
        name=name,
    )(x, x, w_up, conv_w, conv_b, w_down, g, b)


def kernel(x, att_w_in, att_w_out, ret_w_in, ret_w_out, ret_gn_g, ret_gn_b,
           mix_ln_g, mix_ln_b, ffn_ln_g, ffn_ln_b, ffn_w_up, ffn_conv_w,
           ffn_conv_b, ffn_w_down):
    batch, seq, d = x.shape
    h = x.reshape(batch * seq, d)

    def row(p):
        return p.reshape(1, -1)

    def ffn(h, i):
        return _ffn(h, ffn_w_up[i].astype(BF16), ffn_conv_w[i], row(ffn_conv_b[i]),
                    ffn_w_down[i].astype(BF16), row(ffn_ln_g[i]), row(ffn_ln_b[i]),
                    seq, tm=512, tf=256, name=f"conv_ffn_{i}")

    q_scale = jnp.where(jnp.arange(3 * d) < d, ATT_HEAD_DIM ** -0.5, 1.0).astype(F32)
    w_qkv = (att_w_in[0] * q_scale).astype(BF16)
    qkv = _proj(h, w_qkv, tm=1024, tn=1024, name="att_in_proj")
    att = _moba(qkv, batch, seq)
    h = _outproj_ln(att, att_w_out[0].astype(BF16), h, row(mix_ln_g[0]), row(mix_ln_b[0]),
                    tm=512, name="att_out_proj_ln")
    h = ffn(h, 0)

    proj = _proj(h, ret_w_in[0].astype(BF16), tm=1024, tn=1024, name="ret_in_proj")
    ret = _retention(proj, row(ret_gn_g[0]), row(ret_gn_b[0]), batch, seq, chunks_per_step=8)
    h = _outproj_ln(ret, ret_w_out[0].astype(BF16), h, row(mix_ln_g[1]), row(mix_ln_b[1]),
                    tm=512, name="ret_out_proj_ln")
    h = ffn(h, 1)
    return h.reshape(batch, seq, d)
```

```python
import functools
import math

import jax
import jax.numpy as jnp
from jax import lax
from jax.experimental import pallas as pl
from jax.experimental.pallas import tpu as pltpu

F32 = jnp.float32
BF16 = jnp.bfloat16

D_MODEL = 1024
DEPTH = 2
ATT_HEADS = 16
ATT_HEAD_DIM = 64
MOBA_BLOCK = 256
MOBA_TOPK = 3
RET_HEADS = 4
RET_KEY_DIM = 256
RET_VAL_DIM = 512
RET_CHUNK = 128
FFN_DIM = 2816
CONV_WIDTH = 3
LN_EPS = 1e-5
GN_EPS = 1e-5
DEEPNORM_ALPHA = (2.0 * DEPTH) ** 0.25
NEG_INF = -1e30

LANES = 128
HALO_ROWS = 16
VMEM_LIMIT = 56 * 1024 * 1024

_NT = (((1,), (1,)), ((), ()))
_TN = (((0,), (0,)), ((), ()))


def _layer_norm(z, g, b):
    mu = jnp.mean(z, axis=-1, keepdims=True)
    zc = z - mu
    var = jnp.mean(zc * zc, axis=-1, keepdims=True)
    return zc * lax.rsqrt(var + LN_EPS) * g + b


def _proj_kernel(x_ref, w_ref, o_ref):
    o_ref[...] = jnp.dot(x_ref[...].astype(BF16), w_ref[...],
                         preferred_element_type=F32).astype(o_ref.dtype)


def _proj(x, w, tm, tn, name):
    m, k = x.shape
    n = w.shape[1]
    return pl.pallas_call(
        _proj_kernel,
        grid=(m // tm, n // tn),
        in_specs=[pl.BlockSpec((tm, k), lambda i, j: (i, 0)),
                  pl.BlockSpec((k, tn), lambda i, j: (0, j))],
        out_specs=pl.BlockSpec((tm, tn), lambda i, j: (i, j)),
        out_shape=jax.ShapeDtypeStruct((m, n), BF16),
        compiler_params=pltpu.CompilerParams(
            dimension_semantics=("parallel", "parallel"),
            vmem_limit_bytes=VMEM_LIMIT),
        name=name,
    )(x, w)


def _outproj_ln_kernel(a_ref, w_ref, x_ref, g_ref, b_ref, o_ref):
    y = jnp.dot(a_ref[...], w_ref[...], preferred_element_type=F32)
    z = DEEPNORM_ALPHA * x_ref[...] + y
    o_ref[...] = _layer_norm(z, g_ref[...], b_ref[...])


def _outproj_ln(a, w, x, g, b, tm, name):
    m, k = a.shape
    d = w.shape[1]
    return pl.pallas_call(
        _outproj_ln_kernel,
        grid=(m // tm,),
        in_specs=[pl.BlockSpec((tm, k), lambda i: (i, 0)),
                  pl.BlockSpec((k, d), lambda i: (0, 0)),
                  pl.BlockSpec((tm, d), lambda i: (i, 0)),
                  pl.BlockSpec((1, d), lambda i: (0, 0)),
                  pl.BlockSpec((1, d), lambda i: (0, 0))],
        out_specs=pl.BlockSpec((tm, d), lambda i: (i, 0)),
        out_shape=jax.ShapeDtypeStruct((m, d), F32),
        compiler_params=pltpu.CompilerParams(
            dimension_semantics=("parallel",),
            vmem_limit_bytes=VMEM_LIMIT),
        name=name,
    )(a, w, x, g, b)


_EXT_PARTS = 3
_SLOPE_ROW = _EXT_PARTS * 32
_GROUP = 1
_Q_TILES = 2
_LOG2E = math.log2(math.e)
_V_ROWS = ATT_HEAD_DIM + 16


def _ones_pad(n):
    row = lax.broadcasted_iota(jnp.int32, (_V_ROWS - ATT_HEAD_DIM, n), 0)
    return jnp.where(row == 0, 1.0, 0.0)


def _split3(x):
    hi = x.astype(BF16).astype(F32)
    r1 = x - hi
    mid = r1.astype(BF16).astype(F32)
    lo = (r1 - mid).astype(BF16).astype(F32)
    return hi, mid, lo


def _moba_kernel(q_ref, k_ref, v_ref, o_ref, km_ref, kext_ref, vt_ref, lhst_ref,
                 sa_ref, sb_ref, mga_ref, mgb_ref, m_ref, acc_ref, *, n_blk):
    blk = MOBA_BLOCK
    hd = ATT_HEAD_DIM
    n_slots = 2 * _Q_TILES
    nq = n_slots * blk
    gkeys = _GROUP * blk
    hp = pl.program_id(1)
    j0 = _Q_TILES * pl.program_id(2)

    @pl.when(pl.program_id(2) == 0)
    def _():
        lane = lax.broadcasted_iota(jnp.int32, (blk, LANES), 1)
        key_off = lax.broadcasted_iota(jnp.int32, (blk, LANES), 0).astype(F32)
        is_onehot_lane = lane < _SLOPE_ROW
        is_off_lane = (lane >= _SLOPE_ROW) & (lane < _SLOPE_ROW + _EXT_PARTS)
        ones_pad = _ones_pad(blk)
        for n in range(n_blk):
            rows = slice(n * blk, (n + 1) * blk)
            kb = k_ref[rows, :]
            km_ref[n:n + 1, :] = jnp.sum(kb.astype(F32), axis=0, keepdims=True) * (1.0 / blk)
            pat = jnp.where(is_onehot_lane & ((lane & (n_blk - 1)) == n), 1.0,
                            jnp.where(is_off_lane, key_off, 0.0))
            kext_ref[rows, 0:LANES] = kb
            kext_ref[rows, LANES:2 * LANES] = pat.astype(BF16)
            g, c = divmod(n, _GROUP)
            v_t = v_ref[rows, :].astype(F32).T
            for hh in range(2):
                vt_ref[g, hh, :, c * blk:(c + 1) * blk] = jnp.concatenate(
                    [v_t[hh * hd:(hh + 1) * hd, :], ones_pad], axis=0).astype(BF16)

    q_t = q_ref[...].astype(F32).T
    feat = lax.broadcasted_iota(jnp.int32, (LANES, blk), 0)
    pieces = []
    for t in range(_Q_TILES):
        q_tile = q_t[:, t * blk:(t + 1) * blk]
        pieces += [jnp.where(feat < hd, q_tile, 0.0), jnp.where(feat >= hd, q_tile, 0.0)]
    qs_t = jnp.concatenate(pieces, axis=1).astype(BF16)

    km = km_ref[...]
    km_hi = km.astype(BF16)
    km_lo = (km - km_hi.astype(F32)).astype(BF16)
    gate = (jnp.dot(km_hi, qs_t, preferred_element_type=F32)
            + jnp.dot(km_lo, qs_t, preferred_element_type=F32))
    blk_id = lax.broadcasted_iota(jnp.int32, (n_blk, nq), 0)
    qlane = lax.broadcasted_iota(jnp.int32, (1, nq), 1)
    j_q = j0 + qlane // (2 * blk)
    gate = jnp.where(blk_id < j_q, gate, NEG_INF)
    sel = jnp.zeros((n_blk, nq), F32)
    for r in range(MOBA_TOPK):
        mx = jnp.max(gate, axis=0, keepdims=True)
        idx = jnp.min(jnp.where(gate == mx, blk_id, n_blk), axis=0, keepdims=True)
        hit = blk_id == idx
        sel = jnp.where(hit & (r < j_q), 1.0, sel)
        gate = jnp.where(hit, -3e38, gate)

    slope_a = _LOG2E * jnp.exp2(jnp.full((1, 1), -0.5, F32) * (2 * hp + 1).astype(F32))
    slope_b = _LOG2E * jnp.exp2(jnp.full((1, 1), -0.5, F32) * (2 * hp + 2).astype(F32))
    slope_q = jnp.where((qlane // blk) % 2 == 0, slope_a, slope_b)
    q_off = (qlane & (blk - 1)).astype(F32)
    bias = jnp.where(sel > 0.0, -(slope_q * (((j_q - blk_id) * blk).astype(F32) + q_off)), NEG_INF)
    slope_rows = jnp.concatenate(
        list(_split3(slope_q)) + [jnp.zeros((n_blk - _EXT_PARTS, nq), F32)], axis=0)
    ext_t = jnp.concatenate(list(_split3(bias)) + [slope_rows], axis=0)
    lhst_ref[0:LANES, :] = qs_t
    lhst_ref[LANES:2 * LANES, :] = ext_t.astype(BF16)

    rel = (lax.broadcasted_iota(jnp.int32, (blk, blk), 1)
           - lax.broadcasted_iota(jnp.int32, (blk, blk), 0))
    rel_f = rel.astype(F32)
    for t in range(_Q_TILES):
        own = pl.multiple_of((j0 + t) * blk, blk)
        k_own = k_ref[pl.ds(own, blk), :]
        vt_own = v_ref[pl.ds(own, blk), :].astype(F32).T
        for hh, slope_h in enumerate((slope_a, slope_b)):
            slot = 2 * t + hh
            s = jnp.dot(k_own, qs_t[:, slot * blk:(slot + 1) * blk], preferred_element_type=F32)
            s = jnp.where(rel >= 0, s - slope_h * rel_f, NEG_INF)
            m = jnp.max(s, axis=0, keepdims=True)
            p = jnp.exp2(s - m)
            m_ref[slot] = m
            v_aug = jnp.concatenate([vt_own[hh * hd:(hh + 1) * hd, :], _ones_pad(blk)], axis=0)
            acc_ref[slot] = jnp.dot(v_aug.astype(BF16), p.astype(BF16), preferred_element_type=F32)

    def scores(g, s_ref, mg_ref, slot):
        keys = pl.ds(pl.multiple_of(g * gkeys, gkeys), gkeys)
        s = jnp.dot(kext_ref[keys, :], lhst_ref[:, slot * blk:(slot + 1) * blk],
                    preferred_element_type=F32)
        s_ref[slot] = s
        mg_ref[slot] = jnp.max(s, axis=0, keepdims=True)

    def accumulate(g, s_ref, mg_ref, slot):
        m_old = m_ref[slot]
        m_new = jnp.maximum(m_old, mg_ref[slot])
        a = jnp.exp2(m_old - m_new)
        p = jnp.exp2(s_ref[slot] - m_new)
        acc_ref[slot] = a * acc_ref[slot] + jnp.dot(
            vt_ref[g, slot % 2], p.astype(BF16), preferred_element_type=F32)
        m_ref[slot] = m_new

    j_last = j0 + _Q_TILES - 1
    n_iter = (j_last + 2 * _GROUP - 1) // (2 * _GROUP)
    last_tile = n_blk // _GROUP - 1

    for slot in range(n_slots):
        scores(0, sa_ref, mga_ref, slot)

    def body(i, carry):
        for slot in range(n_slots):
            scores(2 * i + 1, sb_ref, mgb_ref, slot)
            accumulate(2 * i, sa_ref, mga_ref, slot)
        for slot in range(n_slots):
            scores(jnp.minimum(2 * i + 2, last_tile), sa_ref, mga_ref, slot)
            accumulate(2 * i + 1, sb_ref, mgb_ref, slot)
        return carry

    lax.fori_loop(0, n_iter, body, 0)
    for t in range(_Q_TILES):
        halves = []
        for hh in range(2):
            acc = acc_ref[2 * t + hh]
            halves.append(acc[0:hd, :] / acc[hd:hd + 1, :])
        out_t = jnp.concatenate(halves, axis=0)
        o_ref[t * blk:(t + 1) * blk, :] = out_t.T.astype(o_ref.dtype)


def _moba(qkv, batch, seq):
    blk = MOBA_BLOCK
    n_blk = seq // blk
    n_pairs = ATT_HEADS // 2
    steps = n_blk // _Q_TILES
    n_slots = 2 * _Q_TILES
    assert n_blk == 32 and n_blk % (2 * _GROUP) == 0 and n_blk % _Q_TILES == 0 and _Q_TILES >= 2
    return pl.pallas_call(
        functools.partial(_moba_kernel, n_blk=n_blk),
        grid=(batch, n_pairs, steps),
        in_specs=[pl.BlockSpec((_Q_TILES * blk, LANES), lambda b, h, j: (b * steps + j, h)),
                  pl.BlockSpec((seq, LANES), lambda b, h, j: (b, n_pairs + h)),
                  pl.BlockSpec((seq, LANES), lambda b, h, j: (b, 2 * n_pairs + h))],
        out_specs=pl.BlockSpec((_Q_TILES * blk, LANES), lambda b, h, j: (b * steps + j, h)),
        out_shape=jax.ShapeDtypeStruct((batch * seq, D_MODEL), BF16),
        scratch_shapes=[pltpu.VMEM((n_blk, LANES), F32),
                        pltpu.VMEM((seq, 2 * LANES), BF16),
                        pltpu.VMEM((n_blk // _GROUP, 2, _V_ROWS, _GROUP * blk), BF16),
                        pltpu.VMEM((2 * LANES, n_slots * blk), BF16),
                        pltpu.VMEM((n_slots, _GROUP * blk, blk), F32),
                        pltpu.VMEM((n_slots, _GROUP * blk, blk), F32),
                        pltpu.VMEM((n_slots, 1, blk), F32),
                        pltpu.VMEM((n_slots, 1, blk), F32),
                        pltpu.VMEM((n_slots, 1, blk), F32),
                        pltpu.VMEM((n_slots, _V_ROWS, blk), F32)],
        compiler_params=pltpu.CompilerParams(
            dimension_semantics=("parallel", "parallel", "arbitrary"),
            vmem_limit_bytes=VMEM_LIMIT),
        name="moba_attention",
    )(qkv, qkv, qkv)


def _retention_kernel(q_ref, k_ref, v_ref, g_ref, din_ref, qd_ref, kd_ref, cd_ref,
                      gng_ref, gnb_ref, o_ref, state_ref, *, chunks):
    c_len = RET_CHUNK

    @pl.when(pl.program_id(2) == 0)
    def _():
        state_ref[...] = jnp.zeros_like(state_ref)

    decay_in = din_ref[0]
    q_decay = qd_ref[0]
    k_decay = kd_ref[0]
    chunk_decay = cd_ref[0]
    gn_g = gng_ref[...]
    gn_b = gnb_ref[...]

    def chunk(c, carry):
        rows = pl.ds(pl.multiple_of(c * c_len, c_len), c_len)
        q = q_ref[rows, :]
        k = k_ref[rows, :]
        v = v_ref[rows, :]
        state = state_ref[...]
        scores = lax.dot_general(q, k, _NT, preferred_element_type=F32) * decay_in
        inner = jnp.dot(scores.astype(BF16), v, preferred_element_type=F32)
        cross = jnp.dot(q, state.astype(BF16), preferred_element_type=F32) * q_decay
        kd = (k.astype(F32) * k_decay).astype(BF16)
        state_ref[...] = chunk_decay * state + lax.dot_general(
            kd, v, _TN, preferred_element_type=F32)
        o = inner + cross
        mu = jnp.mean(o, axis=-1, keepdims=True)
        oc = o - mu
        var = jnp.mean(oc * oc, axis=-1, keepdims=True)
        on = oc * lax.rsqrt(var + GN_EPS) * gn_g + gn_b
        gate = g_ref[rows, :].astype(F32)
        o_ref[rows, :] = (on * (gate * jax.nn.sigmoid(gate))).astype(o_ref.dtype)
        return carry

    lax.fori_loop(0, chunks, chunk, 0, unroll=True)


def _retention(proj, gn_g, gn_b, batch, seq, chunks_per_step):
    h_n, dk, dv, c_len = RET_HEADS, RET_KEY_DIM, RET_VAL_DIM, RET_CHUNK
    rows = chunks_per_step * c_len
    steps = seq // rows
    log_gamma = jnp.log1p(-jnp.exp2(-5.0 - jnp.arange(h_n, dtype=F32)))
    i = jnp.arange(c_len, dtype=F32)
    diff = i[:, None] - i[None, :]
    k_scale = dk ** -0.5
    decay_in = jnp.where(diff >= 0, jnp.exp(log_gamma[:, None, None] * jnp.maximum(diff, 0.0)), 0.0) * k_scale
    q_decay = jnp.exp(log_gamma[:, None] * (i + 1.0))[..., None]
    k_decay = jnp.exp(log_gamma[:, None] * (c_len - 1.0 - i))[..., None] * k_scale
    chunk_decay = jnp.exp(log_gamma * c_len)[:, None, None]
    k_off = h_n * dk // dk
    v_off = 2 * h_n * dk // dv
    g_off = v_off + h_n
    return pl.pallas_call(
        functools.partial(_retention_kernel, chunks=chunks_per_step),
        grid=(batch, h_n, steps),
        in_specs=[pl.BlockSpec((rows, dk), lambda b, h, s: (b * steps + s, h)),
                  pl.BlockSpec((rows, dk), lambda b, h, s: (b * steps + s, k_off + h)),
                  pl.BlockSpec((rows, dv), lambda b, h, s: (b * steps + s, v_off + h)),
                  pl.BlockSpec((rows, dv), lambda b, h, s: (b * steps + s, g_off + h)),
                  pl.BlockSpec((1, c_len, c_len), lambda b, h, s: (h, 0, 0)),
                  pl.BlockSpec((1, c_len, 1), lambda b, h, s: (h, 0, 0)),
                  pl.BlockSpec((1, c_len, 1), lambda b, h, s: (h, 0, 0)),
                  pl.BlockSpec((1, 1, 1), lambda b, h, s: (h, 0, 0)),
                  pl.BlockSpec((1, dv), lambda b, h, s: (0, h)),
                  pl.BlockSpec((1, dv), lambda b, h, s: (0, h))],
        out_specs=pl.BlockSpec((rows, dv), lambda b, h, s: (b * steps + s, h)),
        out_shape=jax.ShapeDtypeStruct((batch * seq, h_n * dv), BF16),
        scratch_shapes=[pltpu.VMEM((dk, dv), F32)],
        compiler_params=pltpu.CompilerParams(
            dimension_semantics=("parallel", "parallel", "arbitrary"),
            vmem_limit_bytes=VMEM_LIMIT),
        name="retention",
    )(proj, proj, proj, proj, decay_in, q_decay, k_decay, chunk_decay, gn_g, gn_b)


def _ffn_kernel(x_ref, halo_ref, wup_ref, cw_ref, cb_ref, wd_ref, g_ref, b_ref, o_ref,
                xcat_ref, u_ref, *, tm, tf, f_dim, tiles_per_seq):
    seq_start = (pl.program_id(0) % tiles_per_seq) == 0
    halo = jnp.where(seq_start, 0.0, halo_ref[...])
    xcat_ref[0:HALO_ROWS, :] = halo.astype(BF16)
    xcat_ref[HALO_ROWS:, :] = x_ref[...].astype(BF16)
    n_chunks = f_dim // tf

    def up(c):
        xc = xcat_ref[...]
        slot = c % 2
        for half in range(2):
            cols = slice(half * f_dim + c * tf, half * f_dim + (c + 1) * tf)
            u_ref[slot, half] = jnp.dot(xc, wup_ref[:, cols], preferred_element_type=F32)

    def conv(c, half):
        cols = slice(half * f_dim + c * tf, half * f_dim + (c + 1) * tf)
        cw = cw_ref[:, cols]
        u = u_ref.at[c % 2, half]
        h = cb_ref[:, cols] + cw[2:3, :] * u[HALO_ROWS:HALO_ROWS + tm, :]
        h = h + cw[0:1, :] * u[HALO_ROWS - 2:HALO_ROWS - 2 + tm, :]
        h = h + cw[1:2, :] * u[HALO_ROWS - 1:HALO_ROWS - 1 + tm, :]
        return h

    up(0)
    acc = None
    for c in range(n_chunks):
        if c + 1 < n_chunks:
            up(c + 1)
        a = conv(c, 0)
        gate = conv(c, 1)
        act = 0.5 * a * (1.0 + lax.erf(a * (1.0 / math.sqrt(2.0)))) * gate
        d = jnp.dot(act.astype(BF16), wd_ref[c * tf:(c + 1) * tf, :], preferred_element_type=F32)
        acc = d if acc is None else acc + d

    z = DEEPNORM_ALPHA * x_ref[...] + acc
    o_ref[...] = _layer_norm(z, g_ref[...], b_ref[...])


def _ffn(x, w_up, conv_w, conv_b, w_down, g, b, seq, tm, tf, name):
    m, d = x.shape
    f_dim = w_down.shape[0]
    halo_blocks_per_tile = tm // HALO_ROWS
    resident = pl.Buffered(1)
    return pl.pallas_call(
        functools.partial(_ffn_kernel, tm=tm, tf=tf, f_dim=f_dim, tiles_per_seq=seq // tm),
        grid=(m // tm,),
        in_specs=[pl.BlockSpec((tm, d), lambda i: (i, 0)),
                  pl.BlockSpec((HALO_ROWS, d),
                               lambda i: (jnp.maximum(i * halo_blocks_per_tile - 1, 0), 0)),
                  pl.BlockSpec((d, 2 * f_dim), lambda i: (0, 0), pipeline_mode=resident),
                  pl.BlockSpec((CONV_WIDTH, 2 * f_dim), lambda i: (0, 0), pipeline_mode=resident),
                  pl.BlockSpec((1, 2 * f_dim), lambda i: (0, 0), pipeline_mode=resident),
                  pl.BlockSpec((f_dim, d), lambda i: (0, 0), pipeline_mode=resident),
                  pl.BlockSpec((1, d), lambda i: (0, 0)),
                  pl.BlockSpec((1, d), lambda i: (0, 0))],
        out_specs=pl.BlockSpec((tm, d), lambda i: (i, 0)),
        out_shape=jax.ShapeDtypeStruct((m, d), F32),
        scratch_shapes=[pltpu.VMEM((tm + HALO_ROWS, d), BF16),
                        pltpu.VMEM((2, 2, tm + HALO_ROWS, tf), F32)],
        compiler_params=pltpu.CompilerParams(
            dimension_semantics=("parallel",),
            vmem_limit_bytes=VMEM_LIMIT),
        name=name,
    )(x, x, w_up, conv_w, conv_b, w_down, g, b)


def kernel(x, att_w_in, att_w_out, ret_w_in, ret_w_out, ret_gn_g, ret_gn_b,
           mix_ln_g, mix_ln_b, ffn_ln_g, ffn_ln_b, ffn_w_up, ffn_conv_w,
           ffn_conv_b, ffn_w_down):
    batch, seq, d = x.shape
    h = x.reshape(batch * seq, d)

    def row(p):
        return p.reshape(1, -1)

    def ffn(h, i):
        return _ffn(h, ffn_w_up[i].astype(BF16), ffn_conv_w[i], row(ffn_conv_b[i]),
                    ffn_w_down[i].astype(BF16), row(ffn_ln_g[i]), row(ffn_ln_b[i]),
                    seq, tm=512, tf=256, name=f"conv_ffn_{i}")

    q_scale = jnp.where(jnp.arange(3 * d) < d, ATT_HEAD_DIM ** -0.5 * _LOG2E, 1.0).astype(F32)
    w_qkv = (att_w_in[0] * q_scale).astype(BF16)
    qkv = _proj(h, w_qkv, tm=1024, tn=1024, name="att_in_proj")
    att = _moba(qkv, batch, seq)
    h = _outproj_ln(att, att_w_out[0].astype(BF16), h, row(mix_ln_g[0]), row(mix_ln_b[0]),
                    tm=512, name="att_out_proj_ln")
    h = ffn(h, 0)

    proj = _proj(h, ret_w_in[0].astype(BF16), tm=1024, tn=1024, name="ret_in_proj")
    ret = _retention(proj, row(ret_gn_g[0]), row(ret_gn_b[0]), batch, seq, chunks_per_step=8)
    h = _outproj_ln(ret, ret_w_out[0].astype(BF16), h, row(mix_ln_g[1]), row(mix_ln_b[1]),
                    tm=512, name="ret_out_proj_ln")
    h = ffn(h, 1)
    return h.reshape(batch, seq, d)
```

```python
import functools
import math

import jax
import jax.numpy as jnp
from jax import lax
from jax.experimental import pallas as pl
from jax.experimental.pallas import tpu as pltpu

F32 = jnp.float32
BF16 = jnp.bfloat16

D_MODEL = 1024
DEPTH = 2
ATT_HEADS = 16
ATT_HEAD_DIM = 64
MOBA_BLOCK = 256
MOBA_TOPK = 3
RET_HEADS = 4
RET_KEY_DIM = 256
RET_VAL_DIM = 512
RET_CHUNK = 256
FFN_DIM = 2816
CONV_WIDTH = 3
LN_EPS = 1e-5
GN_EPS = 1e-5
DEEPNORM_ALPHA = (2.0 * DEPTH) ** 0.25
NEG_INF = -1e30

LANES = 128
HALO_ROWS = 16
VMEM_LIMIT = 56 * 1024 * 1024

_NT = (((1,), (1,)), ((), ()))
_TN = (((0,), (0,)), ((), ()))


def _layer_norm(z, g, b):
    mu = jnp.mean(z, axis=-1, keepdims=True)
    zc = z - mu
    var = jnp.mean(zc * zc, axis=-1, keepdims=True)
    return zc * lax.rsqrt(var + LN_EPS) * g + b


def _proj_kernel(x_ref, w_ref, o_ref):
    o_ref[...] = jnp.dot(x_ref[...].astype(BF16), w_ref[...],
                         preferred_element_type=F32).astype(o_ref.dtype)


def _proj(x, w, tm, tn, name):
    m, k = x.shape
    n = w.shape[1]
    return pl.pallas_call(
        _proj_kernel,
        grid=(m // tm, n // tn),
        in_specs=[pl.BlockSpec((tm, k), lambda i, j: (i, 0)),
                  pl.BlockSpec((k, tn), lambda i, j: (0, j))],
        out_specs=pl.BlockSpec((tm, tn), lambda i, j: (i, j)),
        out_shape=jax.ShapeDtypeStruct((m, n), BF16),
        compiler_params=pltpu.CompilerParams(
            dimension_semantics=("parallel", "parallel"),
            vmem_limit_bytes=VMEM_LIMIT),
        name=name,
    )(x, w)


def _outproj_ln_kernel(a_ref, w_ref, x_ref, g_ref, b_ref, o_ref):
    y = jnp.dot(a_ref[...], w_ref[...], preferred_element_type=F32)
    z = DEEPNORM_ALPHA * x_ref[...] + y
    o_ref[...] = _layer_norm(z, g_ref[...], b_ref[...])


def _outproj_ln(a, w, x, g, b, tm, name):
    m, k = a.shape
    d = w.shape[1]
    return pl.pallas_call(
        _outproj_ln_kernel,
        grid=(m // tm,),
        in_specs=[pl.BlockSpec((tm, k), lambda i: (i, 0)),
                  pl.BlockSpec((k, d), lambda i: (0, 0)),
                  pl.BlockSpec((tm, d), lambda i: (i, 0)),
                  pl.BlockSpec((1, d), lambda i: (0, 0)),
                  pl.BlockSpec((1, d), lambda i: (0, 0))],
        out_specs=pl.BlockSpec((tm, d), lambda i: (i, 0)),
        out_shape=jax.ShapeDtypeStruct((m, d), F32),
        compiler_params=pltpu.CompilerParams(
            dimension_semantics=("parallel",),
            vmem_limit_bytes=VMEM_LIMIT),
        name=name,
    )(a, w, x, g, b)


_EXT_PARTS = 3
_SLOPE_ROW = _EXT_PARTS * 32
_GROUP = 1
_Q_TILES = 2
_LOG2E = math.log2(math.e)
_V_ROWS = ATT_HEAD_DIM + 16


def _ones_pad(n):
    row = lax.broadcasted_iota(jnp.int32, (_V_ROWS - ATT_HEAD_DIM, n), 0)
    return jnp.where(row == 0, 1.0, 0.0)


def _split3(x):
    hi = x.astype(BF16).astype(F32)
    r1 = x - hi
    mid = r1.astype(BF16).astype(F32)
    lo = (r1 - mid).astype(BF16).astype(F32)
    return hi, mid, lo


def _moba_kernel(q_ref, k_ref, v_ref, o_ref, km_ref, kext_ref, vt_ref, lhst_ref,
                 sa_ref, sb_ref, mga_ref, mgb_ref, m_ref, acc_ref, *, n_blk):
    blk = MOBA_BLOCK
    hd = ATT_HEAD_DIM
    n_slots = 2 * _Q_TILES
    nq = n_slots * blk
    gkeys = _GROUP * blk
    hp = pl.program_id(1)
    j0 = _Q_TILES * pl.program_id(2)

    @pl.when(pl.program_id(2) == 0)
    def _():
        lane = lax.broadcasted_iota(jnp.int32, (blk, LANES), 1)
        key_off = lax.broadcasted_iota(jnp.int32, (blk, LANES), 0).astype(F32)
        is_onehot_lane = lane < _SLOPE_ROW
        is_off_lane = (lane >= _SLOPE_ROW) & (lane < _SLOPE_ROW + _EXT_PARTS)
        ones_pad = _ones_pad(blk)
        for n in range(n_blk):
            rows = slice(n * blk, (n + 1) * blk)
            kb = k_ref[rows, :]
            km_ref[n:n + 1, :] = jnp.sum(kb.astype(F32), axis=0, keepdims=True) * (1.0 / blk)
            pat = jnp.where(is_onehot_lane & ((lane & (n_blk - 1)) == n), 1.0,
                            jnp.where(is_off_lane, key_off, 0.0))
            kext_ref[rows, 0:LANES] = kb
            kext_ref[rows, LANES:2 * LANES] = pat.astype(BF16)
            g, c = divmod(n, _GROUP)
            v_t = v_ref[rows, :].astype(F32).T
            for hh in range(2):
                vt_ref[g, hh, :, c * blk:(c + 1) * blk] = jnp.concatenate(
                    [v_t[hh * hd:(hh + 1) * hd, :], ones_pad], axis=0).astype(BF16)

    q_t = q_ref[...].astype(F32).T
    feat = lax.broadcasted_iota(jnp.int32, (LANES, blk), 0)
    pieces = []
    for t in range(_Q_TILES):
        q_tile = q_t[:, t * blk:(t + 1) * blk]
        pieces += [jnp.where(feat < hd, q_tile, 0.0), jnp.where(feat >= hd, q_tile, 0.0)]
    qs_t = jnp.concatenate(pieces, axis=1).astype(BF16)

    km = km_ref[...]
    km_hi = km.astype(BF16)
    km_lo = (km - km_hi.astype(F32)).astype(BF16)
    gate = (jnp.dot(km_hi, qs_t, preferred_element_type=F32)
            + jnp.dot(km_lo, qs_t, preferred_element_type=F32))
    blk_id = lax.broadcasted_iota(jnp.int32, (n_blk, nq), 0)
    qlane = lax.broadcasted_iota(jnp.int32, (1, nq), 1)
    j_q = j0 + qlane // (2 * blk)
    gate = jnp.where(blk_id < j_q, gate, NEG_INF)
    sel = jnp.zeros((n_blk, nq), F32)
    for r in range(MOBA_TOPK):
        mx = jnp.max(gate, axis=0, keepdims=True)
        idx = jnp.min(jnp.where(gate == mx, blk_id, n_blk), axis=0, keepdims=True)
        hit = blk_id == idx
        sel = jnp.where(hit & (r < j_q), 1.0, sel)
        gate = jnp.where(hit, -3e38, gate)

    slope_a = _LOG2E * jnp.exp2(jnp.full((1, 1), -0.5, F32) * (2 * hp + 1).astype(F32))
    slope_b = _LOG2E * jnp.exp2(jnp.full((1, 1), -0.5, F32) * (2 * hp + 2).astype(F32))
    slope_q = jnp.where((qlane // blk) % 2 == 0, slope_a, slope_b)
    q_off = (qlane & (blk - 1)).astype(F32)
    bias = jnp.where(sel > 0.0, -(slope_q * (((j_q - blk_id) * blk).astype(F32) + q_off)), NEG_INF)
    slope_rows = jnp.concatenate(
        list(_split3(slope_q)) + [jnp.zeros((n_blk - _EXT_PARTS, nq), F32)], axis=0)
    ext_t = jnp.concatenate(list(_split3(bias)) + [slope_rows], axis=0)
    lhst_ref[0:LANES, :] = qs_t
    lhst_ref[LANES:2 * LANES, :] = ext_t.astype(BF16)

    rel = (lax.broadcasted_iota(jnp.int32, (blk, blk), 1)
           - lax.broadcasted_iota(jnp.int32, (blk, blk), 0))
    rel_f = rel.astype(F32)
    for t in range(_Q_TILES):
        own = pl.multiple_of((j0 + t) * blk, blk)
        k_own = k_ref[pl.ds(own, blk), :]
        vt_own = v_ref[pl.ds(own, blk), :].astype(F32).T
        for hh, slope_h in enumerate((slope_a, slope_b)):
            slot = 2 * t + hh
            s = jnp.dot(k_own, qs_t[:, slot * blk:(slot + 1) * blk], preferred_element_type=F32)
            s = jnp.where(rel >= 0, s - slope_h * rel_f, NEG_INF)
            m = jnp.max(s, axis=0, keepdims=True)
            p = jnp.exp2(s - m)
            m_ref[slot] = m
            v_aug = jnp.concatenate([vt_own[hh * hd:(hh + 1) * hd, :], _ones_pad(blk)], axis=0)
            acc_ref[slot] = jnp.dot(v_aug.astype(BF16), p.astype(BF16), preferred_element_type=F32)

    def scores(g, s_ref, mg_ref, slot):
        keys = pl.ds(pl.multiple_of(g * gkeys, gkeys), gkeys)
        s = jnp.dot(kext_ref[keys, :], lhst_ref[:, slot * blk:(slot + 1) * blk],
                    preferred_element_type=F32)
        s_ref[slot] = s
        mg_ref[slot] = jnp.max(s, axis=0, keepdims=True)

    def accumulate(g, s_ref, mg_ref, slot):
        m_old = m_ref[slot]
        m_new = jnp.maximum(m_old, mg_ref[slot])
        a = jnp.exp2(m_old - m_new)
        p = jnp.exp2(s_ref[slot] - m_new)
        acc_ref[slot] = a * acc_ref[slot] + jnp.dot(
            vt_ref[g, slot % 2], p.astype(BF16), preferred_element_type=F32)
        m_ref[slot] = m_new

    j_last = j0 + _Q_TILES - 1
    n_iter = (j_last + 2 * _GROUP - 1) // (2 * _GROUP)
    last_tile = n_blk // _GROUP - 1

    for slot in range(n_slots):
        scores(0, sa_ref, mga_ref, slot)

    def body(i, carry):
        for slot in range(n_slots):
            scores(2 * i + 1, sb_ref, mgb_ref, slot)
            accumulate(2 * i, sa_ref, mga_ref, slot)
        for slot in range(n_slots):
            scores(jnp.minimum(2 * i + 2, last_tile), sa_ref, mga_ref, slot)
            accumulate(2 * i + 1, sb_ref, mgb_ref, slot)
        return carry

    lax.fori_loop(0, n_iter, body, 0)
    for t in range(_Q_TILES):
        halves = []
        for hh in range(2):
            acc = acc_ref[2 * t + hh]
            halves.append(acc[0:hd, :] / acc[hd:hd + 1, :])
        out_t = jnp.concatenate(halves, axis=0)
        o_ref[t * blk:(t + 1) * blk, :] = out_t.T.astype(o_ref.dtype)


def _moba(qkv, batch, seq):
    blk = MOBA_BLOCK
    n_blk = seq // blk
    n_pairs = ATT_HEADS // 2
    steps = n_blk // _Q_TILES
    n_slots = 2 * _Q_TILES
    assert n_blk == 32 and n_blk % (2 * _GROUP) == 0 and n_blk % _Q_TILES == 0 and _Q_TILES >= 2
    return pl.pallas_call(
        functools.partial(_moba_kernel, n_blk=n_blk),
        grid=(batch, n_pairs, steps),
        in_specs=[pl.BlockSpec((_Q_TILES * blk, LANES), lambda b, h, j: (b * steps + j, h)),
                  pl.BlockSpec((seq, LANES), lambda b, h, j: (b, n_pairs + h)),
                  pl.BlockSpec((seq, LANES), lambda b, h, j: (b, 2 * n_pairs + h))],
        out_specs=pl.BlockSpec((_Q_TILES * blk, LANES), lambda b, h, j: (b * steps + j, h)),
        out_shape=jax.ShapeDtypeStruct((batch * seq, D_MODEL), BF16),
        scratch_shapes=[pltpu.VMEM((n_blk, LANES), F32),
                        pltpu.VMEM((seq, 2 * LANES), BF16),
                        pltpu.VMEM((n_blk // _GROUP, 2, _V_ROWS, _GROUP * blk), BF16),
                        pltpu.VMEM((2 * LANES, n_slots * blk), BF16),
                        pltpu.VMEM((n_slots, _GROUP * blk, blk), F32),
                        pltpu.VMEM((n_slots, _GROUP * blk, blk), F32),
                        pltpu.VMEM((n_slots, 1, blk), F32),
                        pltpu.VMEM((n_slots, 1, blk), F32),
                        pltpu.VMEM((n_slots, 1, blk), F32),
                        pltpu.VMEM((n_slots, _V_ROWS, blk), F32)],
        compiler_params=pltpu.CompilerParams(
            dimension_semantics=("parallel", "parallel", "arbitrary"),
            vmem_limit_bytes=VMEM_LIMIT),
        name="moba_attention",
    )(qkv, qkv, qkv)


def _retention_kernel(q_ref, k_ref, v_ref, g_ref, din_ref, qd_ref, kd_ref, cd_ref,
                      gng_ref, gnb_ref, o_ref, state_ref, *, chunks):
    c_len = RET_CHUNK

    @pl.when(pl.program_id(2) == 0)
    def _():
        state_ref[...] = jnp.zeros_like(state_ref)

    decay_in = din_ref[0]
    q_decay = qd_ref[0]
    k_decay = kd_ref[0]
    chunk_decay = cd_ref[0]
    gn_g = gng_ref[...]
    gn_b = gnb_ref[...]

    def chunk(c, carry):
        rows = pl.ds(pl.multiple_of(c * c_len, c_len), c_len)
        q = q_ref[rows, :]
        k = k_ref[rows, :]
        v = v_ref[rows, :]
        state = state_ref[...]
        scores = lax.dot_general(q, k, _NT, preferred_element_type=F32) * decay_in
        inner = jnp.dot(scores.astype(BF16), v, preferred_element_type=F32)
        cross = jnp.dot(q, state.astype(BF16), preferred_element_type=F32) * q_decay
        kd = (k.astype(F32) * k_decay).astype(BF16)
        state_ref[...] = chunk_decay * state + lax.dot_general(
            kd, v, _TN, preferred_element_type=F32)
        o = inner + cross
        mu = jnp.mean(o, axis=-1, keepdims=True)
        oc = o - mu
        var = jnp.mean(oc * oc, axis=-1, keepdims=True)
        on = oc * lax.rsqrt(var + GN_EPS) * gn_g + gn_b
        gate = g_ref[rows, :].astype(F32)
        o_ref[rows, :] = (on * (gate * jax.nn.sigmoid(gate))).astype(o_ref.dtype)
        return carry

    lax.fori_loop(0, chunks, chunk, 0, unroll=True)


def _retention(proj, gn_g, gn_b, batch, seq, chunks_per_step):
    h_n, dk, dv, c_len = RET_HEADS, RET_KEY_DIM, RET_VAL_DIM, RET_CHUNK
    rows = chunks_per_step * c_len
    steps = seq // rows
    log_gamma = jnp.log1p(-jnp.exp2(-5.0 - jnp.arange(h_n, dtype=F32)))
    i = jnp.arange(c_len, dtype=F32)
    diff = i[:, None] - i[None, :]
    k_scale = dk ** -0.5
    decay_in = jnp.where(diff >= 0, jnp.exp(log_gamma[:, None, None] * jnp.maximum(diff, 0.0)), 0.0) * k_scale
    q_decay = jnp.exp(log_gamma[:, None] * (i + 1.0))[..., None]
    k_decay = jnp.exp(log_gamma[:, None] * (c_len - 1.0 - i))[..., None] * k_scale
    chunk_decay = jnp.exp(log_gamma * c_len)[:, None, None]
    k_off = h_n * dk // dk
    v_off = 2 * h_n * dk // dv
    g_off = v_off + h_n
    return pl.pallas_call(
        functools.partial(_retention_kernel, chunks=chunks_per_step),
        grid=(batch, h_n, steps),
        in_specs=[pl.BlockSpec((rows, dk), lambda b, h, s: (b * steps + s, h)),
                  pl.BlockSpec((rows, dk), lambda b, h, s: (b * steps + s, k_off + h)),
                  pl.BlockSpec((rows, dv), lambda b, h, s: (b * steps + s, v_off + h)),
                  pl.BlockSpec((rows, dv), lambda b, h, s: (b * steps + s, g_off + h)),
                  pl.BlockSpec((1, c_len, c_len), lambda b, h, s: (h, 0, 0)),
                  pl.BlockSpec((1, c_len, 1), lambda b, h, s: (h, 0, 0)),
                  pl.BlockSpec((1, c_len, 1), lambda b, h, s: (h, 0, 0)),
                  pl.BlockSpec((1, 1, 1), lambda b, h, s: (h, 0, 0)),
                  pl.BlockSpec((1, dv), lambda b, h, s: (0, h)),
                  pl.BlockSpec((1, dv), lambda b, h, s: (0, h))],
        out_specs=pl.BlockSpec((rows, dv), lambda b, h, s: (b * steps + s, h)),
        out_shape=jax.ShapeDtypeStruct((batch * seq, h_n * dv), BF16),
        scratch_shapes=[pltpu.VMEM((dk, dv), F32)],
        compiler_params=pltpu.CompilerParams(
            dimension_semantics=("parallel", "parallel", "arbitrary"),
            vmem_limit_bytes=VMEM_LIMIT),
        name="retention",
    )(proj, proj, proj, proj, decay_in, q_decay, k_decay, chunk_decay, gn_g, gn_b)


_FFN_SLOTS = 11


def _ffn_kernel(x_ref, halo_ref, wup_ref, cw_ref, cb_ref, wd_ref, g_ref, b_ref, o_ref,
                xcat_ref, u_ref, *, tm, tf, f_dim, tiles_per_seq):
    seq_start = (pl.program_id(0) % tiles_per_seq) == 0
    halo = jnp.where(seq_start, 0.0, halo_ref[...])
    xcat_ref[0:HALO_ROWS, :] = halo.astype(BF16)
    xcat_ref[HALO_ROWS:, :] = x_ref[...].astype(BF16)
    n_chunks = f_dim // tf

    def up(c):
        xc = xcat_ref[...]
        slot = c % _FFN_SLOTS
        for half in range(2):
            cols = slice(half * f_dim + c * tf, half * f_dim + (c + 1) * tf)
            u_ref[slot, half] = jnp.dot(xc, wup_ref[:, cols], preferred_element_type=F32)

    def conv(c, half):
        cols = slice(half * f_dim + c * tf, half * f_dim + (c + 1) * tf)
        cw = cw_ref[:, cols]
        u = u_ref.at[c % _FFN_SLOTS, half]
        h = cb_ref[:, cols] + cw[2:3, :] * u[HALO_ROWS:HALO_ROWS + tm, :]
        h = h + cw[0:1, :] * u[HALO_ROWS - 2:HALO_ROWS - 2 + tm, :]
        h = h + cw[1:2, :] * u[HALO_ROWS - 1:HALO_ROWS - 1 + tm, :]
        return h

    for c in range(_FFN_SLOTS - 1):
        up(c)
    acc = None
    for c in range(n_chunks):
        if c + _FFN_SLOTS - 1 < n_chunks:
            up(c + _FFN_SLOTS - 1)
        a = conv(c, 0)
        gate = conv(c, 1)
        act = 0.5 * a * (1.0 + lax.erf(a * (1.0 / math.sqrt(2.0)))) * gate
        d = jnp.dot(act.astype(BF16), wd_ref[c * tf:(c + 1) * tf, :], preferred_element_type=F32)
        acc = d if acc is None else acc + d

    z = DEEPNORM_ALPHA * x_ref[...] + acc
    o_ref[...] = _layer_norm(z, g_ref[...], b_ref[...])


def _ffn(x, w_up, conv_w, conv_b, w_down, g, b, seq, tm, tf, name):
    m, d = x.shape
    f_dim = w_down.shape[0]
    halo_blocks_per_tile = tm // HALO_ROWS
    resident = pl.Buffered(1)
    return pl.pallas_call(
        functools.partial(_ffn_kernel, tm=tm, tf=tf, f_dim=f_dim, tiles_per_seq=seq // tm),
        grid=(m // tm,),
        in_specs=[pl.BlockSpec((tm, d), lambda i: (i, 0)),
                  pl.BlockSpec((HALO_ROWS, d),
                               lambda i: (jnp.maximum(i * halo_blocks_per_tile - 1, 0), 0)),
                  pl.BlockSpec((d, 2 * f_dim), lambda i: (0, 0), pipeline_mode=resident),
                  pl.BlockSpec((CONV_WIDTH, 2 * f_dim), lambda i: (0, 0), pipeline_mode=resident),
                  pl.BlockSpec((1, 2 * f_dim), lambda i: (0, 0), pipeline_mode=resident),
                  pl.BlockSpec((f_dim, d), lambda i: (0, 0), pipeline_mode=resident),
                  pl.BlockSpec((1, d), lambda i: (0, 0)),
                  pl.BlockSpec((1, d), lambda i: (0, 0))],
        out_specs=pl.BlockSpec((tm, d), lambda i: (i, 0)),
        out_shape=jax.ShapeDtypeStruct((m, d), F32),
        scratch_shapes=[pltpu.VMEM((tm + HALO_ROWS, d), BF16),
                        pltpu.VMEM((_FFN_SLOTS, 2, tm + HALO_ROWS, tf), F32)],
        compiler_params=pltpu.CompilerParams(
            dimension_semantics=("parallel",),
            vmem_limit_bytes=VMEM_LIMIT),
        name=name,
    )(x, x, w_up, conv_w, conv_b, w_down, g, b)


def kernel(x, att_w_in, att_w_out, ret_w_in, ret_w_out, ret_gn_g, ret_gn_b,
           mix_ln_g, mix_ln_b, ffn_ln_g, ffn_ln_b, ffn_w_up, ffn_conv_w,
           ffn_conv_b, ffn_w_down):
    batch, seq, d = x.shape
    h = x.reshape(batch * seq, d)

    def row(p):
        return p.reshape(1, -1)

    def ffn(h, i):
        return _ffn(h, ffn_w_up[i].astype(BF16), ffn_conv_w[i], row(ffn_conv_b[i]),
                    ffn_w_down[i].astype(BF16), row(ffn_ln_g[i]), row(ffn_ln_b[i]),
                    seq, tm=512, tf=256, name=f"conv_ffn_{i}")

    q_scale = jnp.where(jnp.arange(3 * d) < d, ATT_HEAD_DIM ** -0.5 * _LOG2E, 1.0).astype(F32)
    w_qkv = (att_w_in[0] * q_scale).astype(BF16)
    qkv = _proj(h, w_qkv, tm=1024, tn=1024, name="att_in_proj")
    att = _moba(qkv, batch, seq)
    h = _outproj_ln(att, att_w_out[0].astype(BF16), h, row(mix_ln_g[0]), row(mix_ln_b[0]),
                    tm=1024, name="att_out_proj_ln")
    h = ffn(h, 0)

    proj = _proj(h, ret_w_in[0].astype(BF16), tm=1024, tn=1024, name="ret_in_proj")
    ret = _retention(proj, row(ret_gn_g[0]), row(ret_gn_b[0]), batch, seq, chunks_per_step=8)
    h = _outproj_ln(ret, ret_w_out[0].astype(BF16), h, row(mix_ln_g[1]), row(mix_ln_b[1]),
                    tm=1024, name="ret_out_proj_ln")
    h = ffn(h, 1)
    return h.reshape(batch, seq, d)
```

```python
import functools
import math

import jax
import jax.numpy as jnp
from jax import lax
from jax.experimental import pallas as pl
from jax.experimental.pallas import tpu as pltpu

F32 = jnp.float32
BF16 = jnp.bfloat16

D_MODEL = 1024
DEPTH = 2
ATT_HEADS = 16
ATT_HEAD_DIM = 64
MOBA_BLOCK = 256
MOBA_TOPK = 3
RET_HEADS = 4
RET_KEY_DIM = 256
RET_VAL_DIM = 512
RET_CHUNK = 256
FFN_DIM = 2816
CONV_WIDTH = 3
LN_EPS = 1e-5
GN_EPS = 1e-5
DEEPNORM_ALPHA = (2.0 * DEPTH) ** 0.25
NEG_INF = -1e30

LANES = 128
HALO_ROWS = 16
VMEM_LIMIT = 56 * 1024 * 1024

_NT = (((1,), (1,)), ((), ()))
_TN = (((0,), (0,)), ((), ()))


def _layer_norm(z, g, b):
    mu = jnp.mean(z, axis=-1, keepdims=True)
    zc = z - mu
    var = jnp.mean(zc * zc, axis=-1, keepdims=True)
    return zc * lax.rsqrt(var + LN_EPS) * g + b


def _proj_kernel(x_ref, w_ref, o_ref):
    o_ref[...] = jnp.dot(x_ref[...].astype(BF16), w_ref[...],
                         preferred_element_type=F32).astype(o_ref.dtype)


def _proj(x, w, tm, tn, name):
    m, k = x.shape
    n = w.shape[1]
    return pl.pallas_call(
        _proj_kernel,
        grid=(m // tm, n // tn),
        in_specs=[pl.BlockSpec((tm, k), lambda i, j: (i, 0)),
                  pl.BlockSpec((k, tn), lambda i, j: (0, j))],
        out_specs=pl.BlockSpec((tm, tn), lambda i, j: (i, j)),
        out_shape=jax.ShapeDtypeStruct((m, n), BF16),
        compiler_params=pltpu.CompilerParams(
            dimension_semantics=("parallel", "parallel"),
            vmem_limit_bytes=VMEM_LIMIT),
        name=name,
    )(x, w)


def _outproj_ln_kernel(a_ref, w_ref, x_ref, g_ref, b_ref, o_ref):
    y = jnp.dot(a_ref[...], w_ref[...], preferred_element_type=F32)
    z = DEEPNORM_ALPHA * x_ref[...] + y
    o_ref[...] = _layer_norm(z, g_ref[...], b_ref[...])


def _outproj_ln(a, w, x, g, b, tm, name):
    m, k = a.shape
    d = w.shape[1]
    return pl.pallas_call(
        _outproj_ln_kernel,
        grid=(m // tm,),
        in_specs=[pl.BlockSpec((tm, k), lambda i: (i, 0)),
                  pl.BlockSpec((k, d), lambda i: (0, 0)),
                  pl.BlockSpec((tm, d), lambda i: (i, 0)),
                  pl.BlockSpec((1, d), lambda i: (0, 0)),
                  pl.BlockSpec((1, d), lambda i: (0, 0))],
        out_specs=pl.BlockSpec((tm, d), lambda i: (i, 0)),
        out_shape=jax.ShapeDtypeStruct((m, d), F32),
        compiler_params=pltpu.CompilerParams(
            dimension_semantics=("parallel",),
            vmem_limit_bytes=VMEM_LIMIT),
        name=name,
    )(a, w, x, g, b)


_EXT_PARTS = 3
_SLOPE_ROW = _EXT_PARTS * 32
_GROUP = 1
_Q_TILES = 2
_LOG2E = math.log2(math.e)
_V_ROWS = ATT_HEAD_DIM + 16


def _ones_pad(n):
    row = lax.broadcasted_iota(jnp.int32, (_V_ROWS - ATT_HEAD_DIM, n), 0)
    return jnp.where(row == 0, 1.0, 0.0)


def _split3(x):
    hi = x.astype(BF16).astype(F32)
    r1 = x - hi
    mid = r1.astype(BF16).astype(F32)
    lo = (r1 - mid).astype(BF16).astype(F32)
    return hi, mid, lo


def _moba_kernel(q_ref, k_ref, v_ref, o_ref, km_ref, kext_ref, vt_ref, lhst_ref,
                 sa_ref, sb_ref, mga_ref, mgb_ref, m_ref, acc_ref, *, n_blk):
    blk = MOBA_BLOCK
    hd = ATT_HEAD_DIM
    n_slots = 2 * _Q_TILES
    nq = n_slots * blk
    gkeys = _GROUP * blk
    hp = pl.program_id(1)
    j0 = _Q_TILES * pl.program_id(2)

    @pl.when(pl.program_id(2) == 0)
    def _():
        lane = lax.broadcasted_iota(jnp.int32, (blk, LANES), 1)
        key_off = lax.broadcasted_iota(jnp.int32, (blk, LANES), 0).astype(F32)
        is_onehot_lane = lane < _SLOPE_ROW
        is_off_lane = (lane >= _SLOPE_ROW) & (lane < _SLOPE_ROW + _EXT_PARTS)
        ones_pad = _ones_pad(blk)
        for n in range(n_blk):
            rows = slice(n * blk, (n + 1) * blk)
            kb = k_ref[rows, :]
            km_ref[n:n + 1, :] = jnp.sum(kb.astype(F32), axis=0, keepdims=True) * (1.0 / blk)
            pat = jnp.where(is_onehot_lane & ((lane & (n_blk - 1)) == n), 1.0,
                            jnp.where(is_off_lane, key_off, 0.0))
            kext_ref[rows, 0:LANES] = kb
            kext_ref[rows, LANES:2 * LANES] = pat.astype(BF16)
            g, c = divmod(n, _GROUP)
            v_t = v_ref[rows, :].astype(F32).T
            for hh in range(2):
                vt_ref[g, hh, :, c * blk:(c + 1) * blk] = jnp.concatenate(
                    [v_t[hh * hd:(hh + 1) * hd, :], ones_pad], axis=0).astype(BF16)

    q_t = q_ref[...].astype(F32).T
    feat = lax.broadcasted_iota(jnp.int32, (LANES, blk), 0)
    pieces = []
    for t in range(_Q_TILES):
        q_tile = q_t[:, t * blk:(t + 1) * blk]
        pieces += [jnp.where(feat < hd, q_tile, 0.0), jnp.where(feat >= hd, q_tile, 0.0)]
    qs_t = jnp.concatenate(pieces, axis=1).astype(BF16)

    km = km_ref[...]
    km_hi = km.astype(BF16)
    km_lo = (km - km_hi.astype(F32)).astype(BF16)
    gate = (jnp.dot(km_hi, qs_t, preferred_element_type=F32)
            + jnp.dot(km_lo, qs_t, preferred_element_type=F32))
    blk_id = lax.broadcasted_iota(jnp.int32, (n_blk, nq), 0)
    qlane = lax.broadcasted_iota(jnp.int32, (1, nq), 1)
    j_q = j0 + qlane // (2 * blk)
    gate = jnp.where(blk_id < j_q, gate, NEG_INF)
    sel = jnp.zeros((n_blk, nq), F32)
    for r in range(MOBA_TOPK):
        mx = jnp.max(gate, axis=0, keepdims=True)
        idx = jnp.min(jnp.where(gate == mx, blk_id, n_blk), axis=0, keepdims=True)
        hit = blk_id == idx
        sel = jnp.where(hit & (r < j_q), 1.0, sel)
        gate = jnp.where(hit, -3e38, gate)

    slope_a = _LOG2E * jnp.exp2(jnp.full((1, 1), -0.5, F32) * (2 * hp + 1).astype(F32))
    slope_b = _LOG2E * jnp.exp2(jnp.full((1, 1), -0.5, F32) * (2 * hp + 2).astype(F32))
    slope_q = jnp.where((qlane // blk) % 2 == 0, slope_a, slope_b)
    q_off = (qlane & (blk - 1)).astype(F32)
    bias = jnp.where(sel > 0.0, -(slope_q * (((j_q - blk_id) * blk).astype(F32) + q_off)), NEG_INF)
    slope_rows = jnp.concatenate(
        list(_split3(slope_q)) + [jnp.zeros((n_blk - _EXT_PARTS, nq), F32)], axis=0)
    ext_t = jnp.concatenate(list(_split3(bias)) + [slope_rows], axis=0)
    lhst_ref[0:LANES, :] = qs_t
    lhst_ref[LANES:2 * LANES, :] = ext_t.astype(BF16)

    rel = (lax.broadcasted_iota(jnp.int32, (blk, blk), 1)
           - lax.broadcasted_iota(jnp.int32, (blk, blk), 0))
    rel_f = rel.astype(F32)
    for t in range(_Q_TILES):
        own = pl.multiple_of((j0 + t) * blk, blk)
        k_own = k_ref[pl.ds(own, blk), :]
        vt_own = v_ref[pl.ds(own, blk), :].astype(F32).T
        for hh, slope_h in enumerate((slope_a, slope_b)):
            slot = 2 * t + hh
            s = jnp.dot(k_own, qs_t[:, slot * blk:(slot + 1) * blk], preferred_element_type=F32)
            s = jnp.where(rel >= 0, s - slope_h * rel_f, NEG_INF)
            m = jnp.max(s, axis=0, keepdims=True)
            p = jnp.exp2(s - m)
            m_ref[slot] = m
            v_aug = jnp.concatenate([vt_own[hh * hd:(hh + 1) * hd, :], _ones_pad(blk)], axis=0)
            acc_ref[slot] = jnp.dot(v_aug.astype(BF16), p.astype(BF16), preferred_element_type=F32)

    def scores(g, s_ref, mg_ref, slot):
        keys = pl.ds(pl.multiple_of(g * gkeys, gkeys), gkeys)
        s = jnp.dot(kext_ref[keys, :], lhst_ref[:, slot * blk:(slot + 1) * blk],
                    preferred_element_type=F32)
        s_ref[slot] = s
        mg_ref[slot] = jnp.max(s, axis=0, keepdims=True)

    def accumulate(g, s_ref, mg_ref, slot):
        m_old = m_ref[slot]
        m_new = jnp.maximum(m_old, mg_ref[slot])
        a = jnp.exp2(m_old - m_new)
        p = jnp.exp2(s_ref[slot] - m_new)
        acc_ref[slot] = a * acc_ref[slot] + jnp.dot(
            vt_ref[g, slot % 2], p.astype(BF16), preferred_element_type=F32)
        m_ref[slot] = m_new

    j_last = j0 + _Q_TILES - 1
    n_iter = (j_last + 4 * _GROUP - 1) // (4 * _GROUP)
    last_tile = n_blk // _GROUP - 1

    for slot in range(n_slots):
        scores(0, sa_ref, mga_ref, slot)

    def body(i, carry):
        for half in range(2):
            first = 4 * i + 2 * half
            for slot in range(n_slots):
                scores(first + 1, sb_ref, mgb_ref, slot)
                accumulate(first, sa_ref, mga_ref, slot)
            for slot in range(n_slots):
                scores(jnp.minimum(first + 2, last_tile), sa_ref, mga_ref, slot)
                accumulate(first + 1, sb_ref, mgb_ref, slot)
        return carry

    lax.fori_loop(0, n_iter, body, 0)
    for t in range(_Q_TILES):
        halves = []
        for hh in range(2):
            acc = acc_ref[2 * t + hh]
            halves.append(acc[0:hd, :] / acc[hd:hd + 1, :])
        out_t = jnp.concatenate(halves, axis=0)
        o_ref[t * blk:(t + 1) * blk, :] = out_t.T.astype(o_ref.dtype)


def _moba(qkv, batch, seq):
    blk = MOBA_BLOCK
    n_blk = seq // blk
    n_pairs = ATT_HEADS // 2
    steps = n_blk // _Q_TILES
    n_slots = 2 * _Q_TILES
    assert n_blk == 32 and n_blk % (4 * _GROUP) == 0 and n_blk % _Q_TILES == 0 and _Q_TILES >= 2
    return pl.pallas_call(
        functools.partial(_moba_kernel, n_blk=n_blk),
        grid=(batch, n_pairs, steps),
        in_specs=[pl.BlockSpec((_Q_TILES * blk, LANES), lambda b, h, j: (b * steps + j, h)),
                  pl.BlockSpec((seq, LANES), lambda b, h, j: (b, n_pairs + h)),
                  pl.BlockSpec((seq, LANES), lambda b, h, j: (b, 2 * n_pairs + h))],
        out_specs=pl.BlockSpec((_Q_TILES * blk, LANES), lambda b, h, j: (b * steps + j, h)),
        out_shape=jax.ShapeDtypeStruct((batch * seq, D_MODEL), BF16),
        scratch_shapes=[pltpu.VMEM((n_blk, LANES), F32),
                        pltpu.VMEM((seq, 2 * LANES), BF16),
                        pltpu.VMEM((n_blk // _GROUP, 2, _V_ROWS, _GROUP * blk), BF16),
                        pltpu.VMEM((2 * LANES, n_slots * blk), BF16),
                        pltpu.VMEM((n_slots, _GROUP * blk, blk), F32),
                        pltpu.VMEM((n_slots, _GROUP * blk, blk), F32),
                        pltpu.VMEM((n_slots, 1, blk), F32),
                        pltpu.VMEM((n_slots, 1, blk), F32),
                        pltpu.VMEM((n_slots, 1, blk), F32),
                        pltpu.VMEM((n_slots, _V_ROWS, blk), F32)],
        compiler_params=pltpu.CompilerParams(
            dimension_semantics=("parallel", "parallel", "arbitrary"),
            vmem_limit_bytes=VMEM_LIMIT),
        name="moba_attention",
    )(qkv, qkv, qkv)


def _retention_kernel(q_ref, k_ref, v_ref, g_ref, din_ref, qd_ref, kd_ref, cd_ref,
                      gng_ref, gnb_ref, o_ref, state_ref, *, chunks):
    c_len = RET_CHUNK

    @pl.when(pl.program_id(2) == 0)
    def _():
        state_ref[...] = jnp.zeros_like(state_ref)

    decay_in = din_ref[0]
    q_decay = qd_ref[0]
    k_decay = kd_ref[0]
    chunk_decay = cd_ref[0]
    gn_g = gng_ref[...]
    gn_b = gnb_ref[...]

    def chunk(c, carry):
        rows = pl.ds(pl.multiple_of(c * c_len, c_len), c_len)
        q = q_ref[rows, :]
        k = k_ref[rows, :]
        v = v_ref[rows, :]
        state = state_ref[...]
        scores = lax.dot_general(q, k, _NT, preferred_element_type=F32) * decay_in
        inner = jnp.dot(scores.astype(BF16), v, preferred_element_type=F32)
        cross = jnp.dot(q, state.astype(BF16), preferred_element_type=F32) * q_decay
        kd = (k.astype(F32) * k_decay).astype(BF16)
        state_ref[...] = chunk_decay * state + lax.dot_general(
            kd, v, _TN, preferred_element_type=F32)
        o = inner + cross
        mu = jnp.mean(o, axis=-1, keepdims=True)
        oc = o - mu
        var = jnp.mean(oc * oc, axis=-1, keepdims=True)
        on = oc * lax.rsqrt(var + GN_EPS) * gn_g + gn_b
        gate = g_ref[rows, :].astype(F32)
        o_ref[rows, :] = (on * (gate * jax.nn.sigmoid(gate))).astype(o_ref.dtype)
        return carry

    lax.fori_loop(0, chunks, chunk, 0, unroll=True)


def _retention(proj, gn_g, gn_b, batch, seq, chunks_per_step):
    h_n, dk, dv, c_len = RET_HEADS, RET_KEY_DIM, RET_VAL_DIM, RET_CHUNK
    rows = chunks_per_step * c_len
    steps = seq // rows
    log_gamma = jnp.log1p(-jnp.exp2(-5.0 - jnp.arange(h_n, dtype=F32)))
    i = jnp.arange(c_len, dtype=F32)
    diff = i[:, None] - i[None, :]
    k_scale = dk ** -0.5
    decay_in = jnp.where(diff >= 0, jnp.exp(log_gamma[:, None, None] * jnp.maximum(diff, 0.0)), 0.0) * k_scale
    q_decay = jnp.exp(log_gamma[:, None] * (i + 1.0))[..., None]
    k_decay = jnp.exp(log_gamma[:, None] * (c_len - 1.0 - i))[..., None] * k_scale
    chunk_decay = jnp.exp(log_gamma * c_len)[:, None, None]
    k_off = h_n * dk // dk
    v_off = 2 * h_n * dk // dv
    g_off = v_off + h_n
    return pl.pallas_call(
        functools.partial(_retention_kernel, chunks=chunks_per_step),
        grid=(batch, h_n, steps),
        in_specs=[pl.BlockSpec((rows, dk), lambda b, h, s: (b * steps + s, h)),
                  pl.BlockSpec((rows, dk), lambda b, h, s: (b * steps + s, k_off + h)),
                  pl.BlockSpec((rows, dv), lambda b, h, s: (b * steps + s, v_off + h)),
                  pl.BlockSpec((rows, dv), lambda b, h, s: (b * steps + s, g_off + h)),
                  pl.BlockSpec((1, c_len, c_len), lambda b, h, s: (h, 0, 0)),
                  pl.BlockSpec((1, c_len, 1), lambda b, h, s: (h, 0, 0)),
                  pl.BlockSpec((1, c_len, 1), lambda b, h, s: (h, 0, 0)),
                  pl.BlockSpec((1, 1, 1), lambda b, h, s: (h, 0, 0)),
                  pl.BlockSpec((1, dv), lambda b, h, s: (0, h)),
                  pl.BlockSpec((1, dv), lambda b, h, s: (0, h))],
        out_specs=pl.BlockSpec((rows, dv), lambda b, h, s: (b * steps + s, h)),
        out_shape=jax.ShapeDtypeStruct((batch * seq, h_n * dv), BF16),
        scratch_shapes=[pltpu.VMEM((dk, dv), F32)],
        compiler_params=pltpu.CompilerParams(
            dimension_semantics=("parallel", "parallel", "arbitrary"),
            vmem_limit_bytes=VMEM_LIMIT),
        name="retention",
    )(proj, proj, proj, proj, decay_in, q_decay, k_decay, chunk_decay, gn_g, gn_b)


_FFN_SLOTS = 11


def _ffn_kernel(x_ref, halo_ref, wup_ref, cw_ref, cb_ref, wd_ref, g_ref, b_ref, o_ref,
                xcat_ref, u_ref, *, tm, tf, f_dim, tiles_per_seq):
    seq_start = (pl.program_id(0) % tiles_per_seq) == 0
    halo = jnp.where(seq_start, 0.0, halo_ref[...])
    xcat_ref[0:HALO_ROWS, :] = halo.astype(BF16)
    xcat_ref[HALO_ROWS:, :] = x_ref[...].astype(BF16)
    n_chunks = f_dim // tf

    def up(c):
        xc = xcat_ref[...]
        slot = c % _FFN_SLOTS
        for half in range(2):
            cols = slice(half * f_dim + c * tf, half * f_dim + (c + 1) * tf)
            u_ref[slot, half] = jnp.dot(xc, wup_ref[:, cols], preferred_element_type=F32)

    def conv(c, half):
        cols = slice(half * f_dim + c * tf, half * f_dim + (c + 1) * tf)
        cw = cw_ref[:, cols]
        u = u_ref.at[c % _FFN_SLOTS, half]
        h = cb_ref[:, cols] + cw[2:3, :] * u[HALO_ROWS:HALO_ROWS + tm, :]
        h = h + cw[0:1, :] * u[HALO_ROWS - 2:HALO_ROWS - 2 + tm, :]
        h = h + cw[1:2, :] * u[HALO_ROWS - 1:HALO_ROWS - 1 + tm, :]
        return h

    for c in range(_FFN_SLOTS - 1):
        up(c)
    acc = None
    for c in range(n_chunks):
        if c + _FFN_SLOTS - 1 < n_chunks:
            up(c + _FFN_SLOTS - 1)
        a = conv(c, 0)
        gate = conv(c, 1)
        act = 0.5 * a * (1.0 + lax.erf(a * (1.0 / math.sqrt(2.0)))) * gate
        d = jnp.dot(act.astype(BF16), wd_ref[c * tf:(c + 1) * tf, :], preferred_element_type=F32)
        acc = d if acc is None else acc + d

    z = DEEPNORM_ALPHA * x_ref[...] + acc
    o_ref[...] = _layer_norm(z, g_ref[...], b_ref[...])


def _ffn(x, w_up, conv_w, conv_b, w_down, g, b, seq, tm, tf, name):
    m, d = x.shape
    f_dim = w_down.shape[0]
    halo_blocks_per_tile = tm // HALO_ROWS
    resident = pl.Buffered(1)
    return pl.pallas_call(
        functools.partial(_ffn_kernel, tm=tm, tf=tf, f_dim=f_dim, tiles_per_seq=seq // tm),
        grid=(m // tm,),
        in_specs=[pl.BlockSpec((tm, d), lambda i: (i, 0)),
                  pl.BlockSpec((HALO_ROWS, d),
                               lambda i: (jnp.maximum(i * halo_blocks_per_tile - 1, 0), 0)),
                  pl.BlockSpec((d, 2 * f_dim), lambda i: (0, 0), pipeline_mode=resident),
                  pl.BlockSpec((CONV_WIDTH, 2 * f_dim), lambda i: (0, 0), pipeline_mode=resident),
                  pl.BlockSpec((1, 2 * f_dim), lambda i: (0, 0), pipeline_mode=resident),
                  pl.BlockSpec((f_dim, d), lambda i: (0, 0), pipeline_mode=resident),
                  pl.BlockSpec((1, d), lambda i: (0, 0)),
                  pl.BlockSpec((1, d), lambda i: (0, 0))],
        out_specs=pl.BlockSpec((tm, d), lambda i: (i, 0)),
        out_shape=jax.ShapeDtypeStruct((m, d), F32),
        scratch_shapes=[pltpu.VMEM((tm + HALO_ROWS, d), BF16),
                        pltpu.VMEM((_FFN_SLOTS, 2, tm + HALO_ROWS, tf), F32)],
        compiler_params=pltpu.CompilerParams(
            dimension_semantics=("parallel",),
            vmem_limit_bytes=VMEM_LIMIT),
        name=name,
    )(x, x, w_up, conv_w, conv_b, w_down, g, b)


def kernel(x, att_w_in, att_w_out, ret_w_in, ret_w_out, ret_gn_g, ret_gn_b,
           mix_ln_g, mix_ln_b, ffn_ln_g, ffn_ln_b, ffn_w_up, ffn_conv_w,
           ffn_conv_b, ffn_w_down):
    batch, seq, d = x.shape
    h = x.reshape(batch * seq, d)

    def row(p):
        return p.reshape(1, -1)

    def ffn(h, i):
        return _ffn(h, ffn_w_up[i].astype(BF16), ffn_conv_w[i], row(ffn_conv_b[i]),
                    ffn_w_down[i].astype(BF16), row(ffn_ln_g[i]), row(ffn_ln_b[i]),
                    seq, tm=512, tf=256, name=f"conv_ffn_{i}")

    q_scale = jnp.where(jnp.arange(3 * d) < d, ATT_HEAD_DIM ** -0.5 * _LOG2E, 1.0).astype(F32)
    w_qkv = (att_w_in[0] * q_scale).astype(BF16)
    qkv = _proj(h, w_qkv, tm=2048, tn=1024, name="att_in_proj")
    att = _moba(qkv, batch, seq)
    h = _outproj_ln(att, att_w_out[0].astype(BF16), h, row(mix_ln_g[0]), row(mix_ln_b[0]),
                    tm=1024, name="att_out_proj_ln")
    h = ffn(h, 0)

    proj = _proj(h, ret_w_in[0].astype(BF16), tm=2048, tn=1024, name="ret_in_proj")
    ret = _retention(proj, row(ret_gn_g[0]), row(ret_gn_b[0]), batch, seq, chunks_per_step=8)
    h = _outproj_ln(ret, ret_w_out[0].astype(BF16), h, row(mix_ln_g[1]), row(mix_ln_b[1]),
                    tm=1024, name="ret_out_proj_ln")
    h = ffn(h, 1)
    return h.reshape(batch, seq, d)
```

```python
import functools
import math

import jax
import jax.numpy as jnp
from jax import lax
from jax.experimental import pallas as pl
from jax.experimental.pallas import tpu as pltpu

F32 = jnp.float32
BF16 = jnp.bfloat16

D_MODEL = 1024
DEPTH = 2
ATT_HEADS = 16
ATT_HEAD_DIM = 64
MOBA_BLOCK = 256
MOBA_TOPK = 3
RET_HEADS = 4
RET_KEY_DIM = 256
RET_VAL_DIM = 512
RET_CHUNK = 256
FFN_DIM = 2816
CONV_WIDTH = 3
LN_EPS = 1e-5
GN_EPS = 1e-5
DEEPNORM_ALPHA = (2.0 * DEPTH) ** 0.25
NEG_INF = -1e30

LANES = 128
HALO_ROWS = 16
VMEM_LIMIT = 56 * 1024 * 1024

_NT = (((1,), (1,)), ((), ()))
_TN = (((0,), (0,)), ((), ()))


def _layer_norm(z, g, b):
    mu = jnp.mean(z, axis=-1, keepdims=True)
    zc = z - mu
    var = jnp.mean(zc * zc, axis=-1, keepdims=True)
    return zc * lax.rsqrt(var + LN_EPS) * g + b


def _proj_kernel(x_ref, w_ref, o_ref):
    o_ref[...] = jnp.dot(x_ref[...].astype(BF16), w_ref[...],
                         preferred_element_type=F32).astype(o_ref.dtype)


def _proj(x, w, tm, tn, name):
    m, k = x.shape
    n = w.shape[1]
    return pl.pallas_call(
        _proj_kernel,
        grid=(m // tm, n // tn),
        in_specs=[pl.BlockSpec((tm, k), lambda i, j: (i, 0)),
                  pl.BlockSpec((k, tn), lambda i, j: (0, j))],
        out_specs=pl.BlockSpec((tm, tn), lambda i, j: (i, j)),
        out_shape=jax.ShapeDtypeStruct((m, n), BF16),
        compiler_params=pltpu.CompilerParams(
            dimension_semantics=("parallel", "parallel"),
            vmem_limit_bytes=VMEM_LIMIT),
        name=name,
    )(x, w)


def _outproj_ln_kernel(a_ref, w_ref, x_ref, g_ref, b_ref, o_ref):
    y = jnp.dot(a_ref[...], w_ref[...], preferred_element_type=F32)
    z = DEEPNORM_ALPHA * x_ref[...] + y
    o_ref[...] = _layer_norm(z, g_ref[...], b_ref[...])


def _outproj_ln(a, w, x, g, b, tm, name):
    m, k = a.shape
    d = w.shape[1]
    return pl.pallas_call(
        _outproj_ln_kernel,
        grid=(m // tm,),
        in_specs=[pl.BlockSpec((tm, k), lambda i: (i, 0)),
                  pl.BlockSpec((k, d), lambda i: (0, 0)),
                  pl.BlockSpec((tm, d), lambda i: (i, 0)),
                  pl.BlockSpec((1, d), lambda i: (0, 0)),
                  pl.BlockSpec((1, d), lambda i: (0, 0))],
        out_specs=pl.BlockSpec((tm, d), lambda i: (i, 0)),
        out_shape=jax.ShapeDtypeStruct((m, d), F32),
        compiler_params=pltpu.CompilerParams(
            dimension_semantics=("parallel",),
            vmem_limit_bytes=VMEM_LIMIT),
        name=name,
    )(a, w, x, g, b)


_EXT_PARTS = 3
_SLOPE_ROW = _EXT_PARTS * 32
_GROUP = 1
_Q_TILES = 2
_LOG2E = math.log2(math.e)
_V_ROWS = ATT_HEAD_DIM + 16


def _ones_pad(n):
    row = lax.broadcasted_iota(jnp.int32, (_V_ROWS - ATT_HEAD_DIM, n), 0)
    return jnp.where(row == 0, 1.0, 0.0)


def _split3(x):
    hi = x.astype(BF16).astype(F32)
    r1 = x - hi
    mid = r1.astype(BF16).astype(F32)
    lo = (r1 - mid).astype(BF16).astype(F32)
    return hi, mid, lo


def _moba_kernel(q_ref, k_ref, v_ref, o_ref, km_ref, kext_ref, vt_ref, lhst_ref,
                 sa_ref, sb_ref, mga_ref, mgb_ref, m_ref, acc_ref, *, n_blk):
    blk = MOBA_BLOCK
    hd = ATT_HEAD_DIM
    n_slots = 2 * _Q_TILES
    nq = n_slots * blk
    gkeys = _GROUP * blk
    hp = pl.program_id(1)
    j0 = _Q_TILES * pl.program_id(2)

    @pl.when(pl.program_id(2) == 0)
    def _():
        lane = lax.broadcasted_iota(jnp.int32, (blk, LANES), 1)
        key_off = lax.broadcasted_iota(jnp.int32, (blk, LANES), 0).astype(F32)
        is_onehot_lane = lane < _SLOPE_ROW
        is_off_lane = (lane >= _SLOPE_ROW) & (lane < _SLOPE_ROW + _EXT_PARTS)
        ones_pad = _ones_pad(blk)
        for n in range(n_blk):
            rows = slice(n * blk, (n + 1) * blk)
            kb = k_ref[rows, :]
            km_ref[n:n + 1, :] = jnp.sum(kb.astype(F32), axis=0, keepdims=True) * (1.0 / blk)
            pat = jnp.where(is_onehot_lane & ((lane & (n_blk - 1)) == n), 1.0,
                            jnp.where(is_off_lane, key_off, 0.0))
            kext_ref[rows, 0:LANES] = kb
            kext_ref[rows, LANES:2 * LANES] = pat.astype(BF16)
            g, c = divmod(n, _GROUP)
            v_t = v_ref[rows, :].astype(F32).T
            for hh in range(2):
                vt_ref[g, hh, :, c * blk:(c + 1) * blk] = jnp.concatenate(
                    [v_t[hh * hd:(hh + 1) * hd, :], ones_pad], axis=0).astype(BF16)

    q_t = q_ref[...].astype(F32).T
    feat = lax.broadcasted_iota(jnp.int32, (LANES, blk), 0)
    pieces = []
    for t in range(_Q_TILES):
        q_tile = q_t[:, t * blk:(t + 1) * blk]
        pieces += [jnp.where(feat < hd, q_tile, 0.0), jnp.where(feat >= hd, q_tile, 0.0)]
    qs_t = jnp.concatenate(pieces, axis=1).astype(BF16)

    km = km_ref[...]
    km_hi = km.astype(BF16)
    km_lo = (km - km_hi.astype(F32)).astype(BF16)
    gate = (jnp.dot(km_hi, qs_t, preferred_element_type=F32)
            + jnp.dot(km_lo, qs_t, preferred_element_type=F32))
    blk_id = lax.broadcasted_iota(jnp.int32, (n_blk, nq), 0)
    qlane = lax.broadcasted_iota(jnp.int32, (1, nq), 1)
    j_q = j0 + qlane // (2 * blk)
    gate = jnp.where(blk_id < j_q, gate, NEG_INF)
    sel = jnp.zeros((n_blk, nq), F32)
    for r in range(MOBA_TOPK):
        mx = jnp.max(gate, axis=0, keepdims=True)
        idx = jnp.min(jnp.where(gate == mx, blk_id, n_blk), axis=0, keepdims=True)
        hit = blk_id == idx
        sel = jnp.where(hit & (r < j_q), 1.0, sel)
        gate = jnp.where(hit, -3e38, gate)

    slope_a = _LOG2E * jnp.exp2(jnp.full((1, 1), -0.5, F32) * (2 * hp + 1).astype(F32))
    slope_b = _LOG2E * jnp.exp2(jnp.full((1, 1), -0.5, F32) * (2 * hp + 2).astype(F32))
    slope_q = jnp.where((qlane // blk) % 2 == 0, slope_a, slope_b)
    q_off = (qlane & (blk - 1)).astype(F32)
    bias = jnp.where(sel > 0.0, -(slope_q * (((j_q - blk_id) * blk).astype(F32) + q_off)), NEG_INF)
    slope_rows = jnp.concatenate(
        list(_split3(slope_q)) + [jnp.zeros((n_blk - _EXT_PARTS, nq), F32)], axis=0)
    ext_t = jnp.concatenate(list(_split3(bias)) + [slope_rows], axis=0)
    lhst_ref[0:LANES, :] = qs_t
    lhst_ref[LANES:2 * LANES, :] = ext_t.astype(BF16)

    rel = (lax.broadcasted_iota(jnp.int32, (blk, blk), 1)
           - lax.broadcasted_iota(jnp.int32, (blk, blk), 0))
    rel_f = rel.astype(F32)
    for t in range(_Q_TILES):
        own = pl.multiple_of((j0 + t) * blk, blk)
        k_own = k_ref[pl.ds(own, blk), :]
        vt_own = v_ref[pl.ds(own, blk), :].astype(F32).T
        for hh, slope_h in enumerate((slope_a, slope_b)):
            slot = 2 * t + hh
            s = jnp.dot(k_own, qs_t[:, slot * blk:(slot + 1) * blk], preferred_element_type=F32)
            s = jnp.where(rel >= 0, s - slope_h * rel_f, NEG_INF)
            m = jnp.max(s, axis=0, keepdims=True)
            p = jnp.exp2(s - m)
            m_ref[slot] = m
            v_aug = jnp.concatenate([vt_own[hh * hd:(hh + 1) * hd, :], _ones_pad(blk)], axis=0)
            acc_ref[slot] = jnp.dot(v_aug.astype(BF16), p.astype(BF16), preferred_element_type=F32)

    def scores(g, s_ref, mg_ref, slot):
        keys = pl.ds(pl.multiple_of(g * gkeys, gkeys), gkeys)
        s = jnp.dot(kext_ref[keys, :], lhst_ref[:, slot * blk:(slot + 1) * blk],
                    preferred_element_type=F32)
        s_ref[slot] = s
        mg_ref[slot] = jnp.max(s, axis=0, keepdims=True)

    def accumulate(g, s_ref, mg_ref, slot):
        m_old = m_ref[slot]
        m_new = jnp.maximum(m_old, mg_ref[slot])
        a = jnp.exp2(m_old - m_new)
        p = jnp.exp2(s_ref[slot] - m_new)
        acc_ref[slot] = a * acc_ref[slot] + jnp.dot(
            vt_ref[g, slot % 2], p.astype(BF16), preferred_element_type=F32)
        m_ref[slot] = m_new

    j_last = j0 + _Q_TILES - 1
    n_iter = (j_last + 4 * _GROUP - 1) // (4 * _GROUP)

    for slot in range(n_slots):
        scores(0, sa_ref, mga_ref, slot)

    def four_tiles(i, prefetch_next):
        for half in range(2):
            first = 4 * i + 2 * half
            for slot in range(n_slots):
                scores(first + 1, sb_ref, mgb_ref, slot)
                accumulate(first, sa_ref, mga_ref, slot)
            for slot in range(n_slots):
                if half == 0 or prefetch_next:
                    scores(first + 2, sa_ref, mga_ref, slot)
                accumulate(first + 1, sb_ref, mgb_ref, slot)

    def body(i, carry):
        four_tiles(i, prefetch_next=True)
        return carry

    lax.fori_loop(0, n_iter - 1, body, 0)
    four_tiles(n_iter - 1, prefetch_next=False)
    for t in range(_Q_TILES):
        halves = []
        for hh in range(2):
            acc = acc_ref[2 * t + hh]
            halves.append(acc[0:hd, :] / acc[hd:hd + 1, :])
        out_t = jnp.concatenate(halves, axis=0)
        o_ref[t * blk:(t + 1) * blk, :] = out_t.T.astype(o_ref.dtype)


def _moba(qkv, batch, seq):
    blk = MOBA_BLOCK
    n_blk = seq // blk
    n_pairs = ATT_HEADS // 2
    steps = n_blk // _Q_TILES
    n_slots = 2 * _Q_TILES
    assert n_blk == 32 and n_blk % (4 * _GROUP) == 0 and n_blk % _Q_TILES == 0 and _Q_TILES >= 2
    return pl.pallas_call(
        functools.partial(_moba_kernel, n_blk=n_blk),
        grid=(batch, n_pairs, steps),
        in_specs=[pl.BlockSpec((_Q_TILES * blk, LANES), lambda b, h, j: (b * steps + j, h)),
                  pl.BlockSpec((seq, LANES), lambda b, h, j: (b, n_pairs + h)),
                  pl.BlockSpec((seq, LANES), lambda b, h, j: (b, 2 * n_pairs + h))],
        out_specs=pl.BlockSpec((_Q_TILES * blk, LANES), lambda b, h, j: (b * steps + j, h)),
        out_shape=jax.ShapeDtypeStruct((batch * seq, D_MODEL), BF16),
        scratch_shapes=[pltpu.VMEM((n_blk, LANES), F32),
                        pltpu.VMEM((seq, 2 * LANES), BF16),
                        pltpu.VMEM((n_blk // _GROUP, 2, _V_ROWS, _GROUP * blk), BF16),
                        pltpu.VMEM((2 * LANES, n_slots * blk), BF16),
                        pltpu.VMEM((n_slots, _GROUP * blk, blk), F32),
                        pltpu.VMEM((n_slots, _GROUP * blk, blk), F32),
                        pltpu.VMEM((n_slots, 1, blk), F32),
                        pltpu.VMEM((n_slots, 1, blk), F32),
                        pltpu.VMEM((n_slots, 1, blk), F32),
                        pltpu.VMEM((n_slots, _V_ROWS, blk), F32)],
        compiler_params=pltpu.CompilerParams(
            dimension_semantics=("parallel", "parallel", "arbitrary"),
            vmem_limit_bytes=VMEM_LIMIT),
        name="moba_attention",
    )(qkv, qkv, qkv)


def _retention_kernel(q_ref, k_ref, v_ref, g_ref, din_ref, qd_ref, kd_ref, cd_ref,
                      gng_ref, gnb_ref, o_ref, state_ref, *, chunks):
    c_len = RET_CHUNK

    @pl.when(pl.program_id(2) == 0)
    def _():
        state_ref[...] = jnp.zeros_like(state_ref)

    decay_in = din_ref[0]
    q_decay = qd_ref[0]
    k_decay = kd_ref[0]
    chunk_decay = cd_ref[0]
    gn_g = gng_ref[...]
    gn_b = gnb_ref[...]

    def chunk(c, carry):
        rows = pl.ds(pl.multiple_of(c * c_len, c_len), c_len)
        q = q_ref[rows, :]
        k = k_ref[rows, :]
        v = v_ref[rows, :]
        state = state_ref[...]
        scores = lax.dot_general(q, k, _NT, preferred_element_type=F32) * decay_in
        inner = jnp.dot(scores.astype(BF16), v, preferred_element_type=F32)
        cross = jnp.dot(q, state.astype(BF16), preferred_element_type=F32) * q_decay
        kd = (k.astype(F32) * k_decay).astype(BF16)
        state_ref[...] = chunk_decay * state + lax.dot_general(
            kd, v, _TN, preferred_element_type=F32)
        o = inner + cross
        mu = jnp.mean(o, axis=-1, keepdims=True)
        oc = o - mu
        var = jnp.mean(oc * oc, axis=-1, keepdims=True)
        on = oc * lax.rsqrt(var + GN_EPS) * gn_g + gn_b
        gate = g_ref[rows, :].astype(F32)
        o_ref[rows, :] = (on * (gate * jax.nn.sigmoid(gate))).astype(o_ref.dtype)
        return carry

    lax.fori_loop(0, chunks, chunk, 0, unroll=True)


def _retention(proj, gn_g, gn_b, batch, seq, chunks_per_step):
    h_n, dk, dv, c_len = RET_HEADS, RET_KEY_DIM, RET_VAL_DIM, RET_CHUNK
    rows = chunks_per_step * c_len
    steps = seq // rows
    log_gamma = jnp.log1p(-jnp.exp2(-5.0 - jnp.arange(h_n, dtype=F32)))
    i = jnp.arange(c_len, dtype=F32)
    diff = i[:, None] - i[None, :]
    k_scale = dk ** -0.5
    decay_in = jnp.where(diff >= 0, jnp.exp(log_gamma[:, None, None] * jnp.maximum(diff, 0.0)), 0.0) * k_scale
    q_decay = jnp.exp(log_gamma[:, None] * (i + 1.0))[..., None]
    k_decay = jnp.exp(log_gamma[:, None] * (c_len - 1.0 - i))[..., None] * k_scale
    chunk_decay = jnp.exp(log_gamma * c_len)[:, None, None]
    k_off = h_n * dk // dk
    v_off = 2 * h_n * dk // dv
    g_off = v_off + h_n
    return pl.pallas_call(
        functools.partial(_retention_kernel, chunks=chunks_per_step),
        grid=(batch, h_n, steps),
        in_specs=[pl.BlockSpec((rows, dk), lambda b, h, s: (b * steps + s, h)),
                  pl.BlockSpec((rows, dk), lambda b, h, s: (b * steps + s, k_off + h)),
                  pl.BlockSpec((rows, dv), lambda b, h, s: (b * steps + s, v_off + h)),
                  pl.BlockSpec((rows, dv), lambda b, h, s: (b * steps + s, g_off + h)),
                  pl.BlockSpec((1, c_len, c_len), lambda b, h, s: (h, 0, 0)),
                  pl.BlockSpec((1, c_len, 1), lambda b, h, s: (h, 0, 0)),
                  pl.BlockSpec((1, c_len, 1), lambda b, h, s: (h, 0, 0)),
                  pl.BlockSpec((1, 1, 1), lambda b, h, s: (h, 0, 0)),
                  pl.BlockSpec((1, dv), lambda b, h, s: (0, h)),
                  pl.BlockSpec((1, dv), lambda b, h, s: (0, h))],
        out_specs=pl.BlockSpec((rows, dv), lambda b, h, s: (b * steps + s, h)),
        out_shape=jax.ShapeDtypeStruct((batch * seq, h_n * dv), BF16),
        scratch_shapes=[pltpu.VMEM((dk, dv), F32)],
        compiler_params=pltpu.CompilerParams(
            dimension_semantics=("parallel", "parallel", "arbitrary"),
            vmem_limit_bytes=VMEM_LIMIT),
        name="retention",
    )(proj, proj, proj, proj, decay_in, q_decay, k_decay, chunk_decay, gn_g, gn_b)


_FFN_SLOTS = 11


def _ffn_kernel(x_ref, halo_ref, wup_ref, cw_ref, cb_ref, wd_ref, g_ref, b_ref, o_ref,
                xcat_ref, u_ref, *, tm, tf, f_dim, tiles_per_seq):
    seq_start = (pl.program_id(0) % tiles_per_seq) == 0
    halo = jnp.where(seq_start, 0.0, halo_ref[...])
    xcat_ref[0:HALO_ROWS, :] = halo.astype(BF16)
    xcat_ref[HALO_ROWS:, :] = x_ref[...].astype(BF16)
    n_chunks = f_dim // tf

    def up(c):
        xc = xcat_ref[...]
        slot = c % _FFN_SLOTS
        for half in range(2):
            cols = slice(half * f_dim + c * tf, half * f_dim + (c + 1) * tf)
            u_ref[slot, half] = jnp.dot(xc, wup_ref[:, cols], preferred_element_type=F32)

    def conv(c, half):
        cols = slice(half * f_dim + c * tf, half * f_dim + (c + 1) * tf)
        cw = cw_ref[:, cols]
        u = u_ref.at[c % _FFN_SLOTS, half]
        h = cb_ref[:, cols] + cw[2:3, :] * u[HALO_ROWS:HALO_ROWS + tm, :]
        h = h + cw[0:1, :] * u[HALO_ROWS - 2:HALO_ROWS - 2 + tm, :]
        h = h + cw[1:2, :] * u[HALO_ROWS - 1:HALO_ROWS - 1 + tm, :]
        return h

    for c in range(_FFN_SLOTS - 1):
        up(c)
    acc = None
    for c in range(n_chunks):
        if c + _FFN_SLOTS - 1 < n_chunks:
            up(c + _FFN_SLOTS - 1)
        a = conv(c, 0)
        gate = conv(c, 1)
        act = 0.5 * a * (1.0 + lax.erf(a * (1.0 / math.sqrt(2.0)))) * gate
        d = jnp.dot(act.astype(BF16), wd_ref[c * tf:(c + 1) * tf, :], preferred_element_type=F32)
        acc = d if acc is None else acc + d

    z = DEEPNORM_ALPHA * x_ref[...] + acc
    o_ref[...] = _layer_norm(z, g_ref[...], b_ref[...])


def _ffn(x, w_up, conv_w, conv_b, w_down, g, b, seq, tm, tf, name):
    m, d = x.shape
    f_dim = w_down.shape[0]
    halo_blocks_per_tile = tm // HALO_ROWS
    resident = pl.Buffered(1)
    return pl.pallas_call(
        functools.partial(_ffn_kernel, tm=tm, tf=tf, f_dim=f_dim, tiles_per_seq=seq // tm),
        grid=(m // tm,),
        in_specs=[pl.BlockSpec((tm, d), lambda i: (i, 0)),
                  pl.BlockSpec((HALO_ROWS, d),
                               lambda i: (jnp.maximum(i * halo_blocks_per_tile - 1, 0), 0)),
                  pl.BlockSpec((d, 2 * f_dim), lambda i: (0, 0), pipeline_mode=resident),
                  pl.BlockSpec((CONV_WIDTH, 2 * f_dim), lambda i: (0, 0), pipeline_mode=resident),
                  pl.BlockSpec((1, 2 * f_dim), lambda i: (0, 0), pipeline_mode=resident),
                  pl.BlockSpec((f_dim, d), lambda i: (0, 0), pipeline_mode=resident),
                  pl.BlockSpec((1, d), lambda i: (0, 0)),
                  pl.BlockSpec((1, d), lambda i: (0, 0))],
        out_specs=pl.BlockSpec((tm, d), lambda i: (i, 0)),
        out_shape=jax.ShapeDtypeStruct((m, d), F32),
        scratch_shapes=[pltpu.VMEM((tm + HALO_ROWS, d), BF16),
                        pltpu.VMEM((_FFN_SLOTS, 2, tm + HALO_ROWS, tf), F32)],
        compiler_params=pltpu.CompilerParams(
            dimension_semantics=("parallel",),
            vmem_limit_bytes=VMEM_LIMIT),
        name=name,
    )(x, x, w_up, conv_w, conv_b, w_down, g, b)


def kernel(x, att_w_in, att_w_out, ret_w_in, ret_w_out, ret_gn_g, ret_gn_b,
           mix_ln_g, mix_ln_b, ffn_ln_g, ffn_ln_b, ffn_w_up, ffn_conv_w,
           ffn_conv_b, ffn_w_down):
    batch, seq, d = x.shape
    h = x.reshape(batch * seq, d)

    def row(p):
        return p.reshape(1, -1)

    def ffn(h, i):
        return _ffn(h, ffn_w_up[i].astype(BF16), ffn_conv_w[i], row(ffn_conv_b[i]),
                    ffn_w_down[i].astype(BF16), row(ffn_ln_g[i]), row(ffn_ln_b[i]),
                    seq, tm=512, tf=256, name=f"conv_ffn_{i}")

    q_scale = jnp.where(jnp.arange(3 * d) < d, ATT_HEAD_DIM ** -0.5 * _LOG2E, 1.0).astype(F32)
    w_qkv = (att_w_in[0] * q_scale).astype(BF16)
    qkv = _proj(h, w_qkv, tm=2048, tn=1024, name="att_in_proj")
    att = _moba(qkv, batch, seq)
    h = _outproj_ln(att, att_w_out[0].astype(BF16), h, row(mix_ln_g[0]), row(mix_ln_b[0]),
                    tm=1024, name="att_out_proj_ln")
    h = ffn(h, 0)

    proj = _proj(h, ret_w_in[0].astype(BF16), tm=2048, tn=1024, name="ret_in_proj")
    ret = _retention(proj, row(ret_gn_g[0]), row(ret_gn_b[0]), batch, seq, chunks_per_step=8)
    h = _outproj_ln(ret, ret_w_out[0].astype(BF16), h, row(mix_ln_g[1]), row(mix_ln_b[1]),
                    tm=1024, name="ret_out_proj_ln")
    h = ffn(h, 1)
    return h.reshape(batch, seq, d)
```
